```python
import math
import jax
import jax.numpy as jnp
from jax import lax
import numpy as np

D_MODEL = 1024
BATCH = 4
SEQ = 8192
DEPTH = 2

CHUNK = 64
Q_BLOCK = 128
MEM_LEN = 256
MEM_HEADS = 4
MEM_DIM = D_MODEL // MEM_HEADS

FOX_DIM = 64
FOX_W = D_MODEL // 4
FOX_HEADS = FOX_W // FOX_DIM
DIFF_QK_DIM = 32
DIFF_V_DIM = 2 * DIFF_QK_DIM
DIFF_W = D_MODEL // 4
DIFF_HEADS = DIFF_W // DIFF_V_DIM
HGRN_DIM = 128
HGRN_W = D_MODEL - FOX_W - DIFF_W
HGRN_HEADS = HGRN_W // HGRN_DIM

MIX_W = FOX_W + DIFF_W + HGRN_W
IN_SIZES = (FOX_W, FOX_W, FOX_W, FOX_HEADS, DIFF_W, DIFF_W, DIFF_W, HGRN_W, HGRN_W, HGRN_W, HGRN_W)
IN_W = sum(IN_SIZES)

N_EXPERTS = 16
N_GROUPS = 4
EXPERTS_PER_GROUP = N_EXPERTS // N_GROUPS
TOP_K = 2
EXPERT_FF = D_MODEL // 2

ALPHA = (2 * DEPTH) ** 0.25
BETA = (8 * DEPTH) ** -0.25
LN_EPS = 1e-5

kernel_name = 'hybrid_fox_diff_hgrn2_moe_encoder'


def _layer_norm(x, g, b):
    xf = x.astype(jnp.float32)
    mu = jnp.mean(xf, axis=-1, keepdims=True)
    var = jnp.mean(jnp.square(xf - mu), axis=-1, keepdims=True)
    y = (xf - mu) * lax.rsqrt(var + LN_EPS) * g.astype(jnp.float32) + b.astype(jnp.float32)
    return y.astype(x.dtype)


def _rms_norm(x, g):
    xf = x.astype(jnp.float32)
    y = xf * lax.rsqrt(jnp.mean(xf * xf, axis=-1, keepdims=True) + LN_EPS) * g.astype(jnp.float32)
    return y.astype(x.dtype)


def _split_heads(t, n_heads):
    b, s, _ = t.shape
    return t.reshape(b, s, n_heads, -1).transpose(0, 2, 1, 3)


def _merge_heads(t):
    b, h, s, d = t.shape
    return t.transpose(0, 2, 1, 3).reshape(b, s, h * d)


def _to_blocks(t):
    b, h, s = t.shape[:3]
    t = t.reshape((b, h, s // Q_BLOCK, Q_BLOCK) + t.shape[3:])
    return jnp.moveaxis(t, 2, 0)


def _from_blocks(t):
    nb, b, h, q, d = t.shape
    return t.transpose(1, 2, 0, 3, 4).reshape(b, h, nb * q, d)


def _split_columns(proj):
    parts, start = [], 0
    for width in IN_SIZES:
        parts.append(proj[..., start:start + width])
        start += width
    return parts


def _forgetting_attention(q, k, v, log_f):
    s = q.shape[2]
    cum = jnp.cumsum(log_f, axis=-1)
    key_pos = jnp.arange(s)
    query_pos = key_pos.reshape(s // Q_BLOCK, Q_BLOCK)
    scale = FOX_DIM ** -0.5

    def block(args):
        q_blk, c_blk, q_pos = args
        logits = jnp.einsum('bhqd,bhkd->bhqk', q_blk, k).astype(jnp.float32) * scale
        logits = logits + c_blk[..., :, None] - cum[..., None, :]
        logits = jnp.where(key_pos[None, :] <= q_pos[:, None], logits, -jnp.inf)
        p = jax.nn.softmax(logits, axis=-1).astype(v.dtype)
        return jnp.einsum('bhqk,bhkd->bhqd', p, v)

    out = lax.map(block, (_to_blocks(q), _to_blocks(cum), query_pos))
    return _from_blocks(out)


def _differential_attention(q1, q2, k1, k2, v, lam):
    s = q1.shape[2]
    key_chunk = jnp.arange(s) // CHUNK
    query_chunk = key_chunk.reshape(s // Q_BLOCK, Q_BLOCK)
    scale = DIFF_QK_DIM ** -0.5

    def block(args):
        q1b, q2b, qc = args
        visible = key_chunk[None, :] <= qc[:, None]

        def attn_map(qb, kk):
            logits = jnp.einsum('bhqd,bhkd->bhqk', qb, kk).astype(jnp.float32) * scale
            return jax.nn.softmax(jnp.where(visible, logits, -jnp.inf), axis=-1)

        p = attn_map(q1b, k1) - lam * attn_map(q2b, k2)
        return jnp.einsum('bhqk,bhkd->bhqd', p.astype(v.dtype), v)

    out = lax.map(block, (_to_blocks(q1), _to_blocks(q2), query_chunk))
    return _from_blocks(out)


def _hgrn2_recurrence(q, k, log_f, v):
    b, h, s, dk = q.shape
    dv = v.shape[-1]
    n = s // CHUNK

    def chunks(t):
        t = t.astype(jnp.float32)
        return t.reshape(b, h, n, CHUNK, t.shape[-1]).transpose(2, 0, 1, 3, 4)

    pos = jnp.arange(CHUNK)
    causal = (pos[:, None] >= pos[None, :])[None, None, :, :, None]

    def step(state, inp):
        qc, kc, gc, vc = inp
        bc = jnp.cumsum(gc, axis=2)
        rel = jnp.where(causal, bc[:, :, :, None, :] - bc[:, :, None, :, :], -jnp.inf)
        scores = jnp.einsum('bhtd,bhsd,bhtsd->bhts', qc, kc, jnp.exp(rel))
        o = (jnp.einsum('bhts,bhsv->bhtv', scores, vc)
             + jnp.einsum('bhtd,bhdv->bhtv', qc * jnp.exp(bc), state))
        b_last = bc[:, :, -1, :]
        k_dec = kc * jnp.exp(b_last[:, :, None, :] - bc)
        state = jnp.exp(b_last)[..., None] * state + jnp.einsum('bhsd,bhsv->bhdv', k_dec, vc)
        return state, o

    state0 = jnp.zeros((b, h, dk, dv), jnp.float32)
    _, o = lax.scan(step, state0, (chunks(q), chunks(k), chunks(log_f), chunks(v)))
    return o.transpose(1, 2, 0, 3, 4).reshape(b, h, s, dv)


def _hybrid_mixer(x, w_in, fox_fb, lam_q1, lam_k1, lam_q2, lam_k2, lam_init,
                  diff_norm_g, lower_bound, hgrn_norm_g, w_out):
    proj = x @ w_in
    fq, fk, fv, ff, dq, dk, dv, hq, hf, hi, hg = _split_columns(proj)

    log_f = jax.nn.log_sigmoid(ff.astype(jnp.float32) + fox_fb.astype(jnp.float32))
    y_fox = _forgetting_attention(_split_heads(fq, FOX_HEADS), _split_heads(fk, FOX_HEADS),
                                  _split_heads(fv, FOX_HEADS), log_f.transpose(0, 2, 1))

    dq_h = _split_heads(dq, DIFF_HEADS)
    dk_h = _split_heads(dk, DIFF_HEADS)
    lam = (jnp.exp(jnp.sum(lam_q1.astype(jnp.float32) * lam_k1.astype(jnp.float32)))
           - jnp.exp(jnp.sum(lam_q2.astype(jnp.float32) * lam_k2.astype(jnp.float32)))
           + lam_init)
    y_diff = _differential_attention(dq_h[..., :DIFF_QK_DIM], dq_h[..., DIFF_QK_DIM:],
                                     dk_h[..., :DIFF_QK_DIM], dk_h[..., DIFF_QK_DIM:],
                                     _split_heads(dv, DIFF_HEADS), lam)
    y_diff = _rms_norm(y_diff, diff_norm_g) * (1.0 - lam_init)

    f_logit = hf.astype(jnp.float32)
    lb = lower_bound.astype(jnp.float32)
    log_f_h = jnp.logaddexp(jnp.log(lb), jnp.log1p(-lb) + jax.nn.log_sigmoid(f_logit))
    k_h = (1.0 - lb) * jax.nn.sigmoid(-f_logit)
    o = _hgrn2_recurrence(_split_heads(jax.nn.silu(hq), HGRN_HEADS), _split_heads(k_h, HGRN_HEADS),
                          _split_heads(log_f_h, HGRN_HEADS), _split_heads(hi, HGRN_HEADS))
    y_hgrn = _rms_norm(o, hgrn_norm_g) * jax.nn.silu(_split_heads(hg, HGRN_HEADS).astype(jnp.float32))

    y = jnp.concatenate([_merge_heads(y_fox).astype(x.dtype), _merge_heads(y_diff).astype(x.dtype),
                         _merge_heads(y_hgrn).astype(x.dtype)], axis=-1)
    return y @ w_out


def _memory_attention(x, mem, w_q, w_k, w_v, w_o):
    q = _split_heads(x @ w_q, MEM_HEADS)
    k = _split_heads(mem @ w_k, MEM_HEADS)
    v = _split_heads(mem @ w_v, MEM_HEADS)
    logits = jnp.einsum('bhqd,bhkd->bhqk', q, k).astype(jnp.float32) * MEM_DIM ** -0.5
    p = jax.nn.softmax(logits, axis=-1).astype(v.dtype)
    return _merge_heads(jnp.einsum('bhqk,bhkd->bhqd', p, v)) @ w_o


def _grouped_moe(x, router_w, router_b, w1, w3, w2):
    b, s, d = x.shape
    xt = x.reshape(b * s, d)
    n = xt.shape[0]
    scores = jax.nn.sigmoid((xt @ router_w).astype(jnp.float32))
    biased = scores + router_b.astype(jnp.float32)
    grouped = biased.reshape(n, N_GROUPS, EXPERTS_PER_GROUP)
    group_score = jnp.sum(lax.top_k(grouped, TOP_K)[0], axis=-1)
    group_sel = jnp.argmax(group_score, axis=-1)
    in_group = jnp.arange(N_GROUPS)[None, :] == group_sel[:, None]
    masked = jnp.where(in_group[:, :, None], grouped, -jnp.inf).reshape(n, N_EXPERTS)
    _, expert_idx = lax.top_k(masked, TOP_K)
    sel = jnp.take_along_axis(scores, expert_idx, axis=-1)
    gates = sel / jnp.sum(sel, axis=-1, keepdims=True)
    gate_dense = jnp.sum(jax.nn.one_hot(expert_idx, N_EXPERTS, dtype=jnp.float32) * gates[..., None], axis=1)
    out = jnp.zeros_like(xt)
    for e in range(N_EXPERTS):
        hid = jax.nn.silu(xt @ w1[e]) * (xt @ w3[e])
        out = out + gate_dense[:, e, None].astype(xt.dtype) * (hid @ w2[e])
    return out.reshape(b, s, d)


def setup_inputs(seed: int = 0) -> dict:
    key = jax.random.key(seed)
    ks = jax.random.split(key, 26)
    f32 = jnp.float32

    def nrm(k, shape, scale):
        return jax.random.normal(k, shape, f32) * scale

    d = D_MODEL
    return {
        'x': nrm(ks[0], (BATCH, SEQ, d), 1.0),
        'mem': nrm(ks[1], (BATCH, MEM_LEN, d), 1.0),
        'ln_in_g': 1.0 + nrm(ks[2], (d,), 0.02),
        'ln_in_b': nrm(ks[3], (d,), 0.02),
        'w_in': nrm(ks[4], (DEPTH, d, IN_W), d ** -0.5),
        'fox_fb': jax.random.uniform(ks[5], (DEPTH, FOX_HEADS), f32, 1.0, 3.0),
        'lam_q1': nrm(ks[6], (DEPTH, DIFF_QK_DIM), 0.1),
        'lam_k1': nrm(ks[7], (DEPTH, DIFF_QK_DIM), 0.1),
        'lam_q2': nrm(ks[8], (DEPTH, DIFF_QK_DIM), 0.1),
        'lam_k2': nrm(ks[9], (DEPTH, DIFF_QK_DIM), 0.1),
        'diff_norm_g': 1.0 + nrm(ks[10], (DEPTH, DIFF_V_DIM), 0.02),
        'hgrn_lb': nrm(ks[11], (DEPTH, HGRN_W), 0.1),
        'hgrn_norm_g': 1.0 + nrm(ks[12], (DEPTH, HGRN_DIM), 0.02),
        'w_out': nrm(ks[13], (DEPTH, MIX_W, d), BETA * MIX_W ** -0.5),
        'mem_wq': nrm(ks[14], (DEPTH, d, d), d ** -0.5),
        'mem_wk': nrm(ks[15], (DEPTH, d, d), d ** -0.5),
        'mem_wv': nrm(ks[16], (DEPTH, d, d), d ** -0.5),
        'mem_wo': nrm(ks[17], (DEPTH, d, d), BETA * d ** -0.5),
        'router_w': nrm(ks[18], (d, N_EXPERTS), d ** -0.5),
        'router_b': nrm(ks[19], (N_EXPERTS,), 0.01),
        'w1': nrm(ks[20], (DEPTH, N_EXPERTS, d, EXPERT_FF), d ** -0.5),
        'w3': nrm(ks[21], (DEPTH, N_EXPERTS, d, EXPERT_FF), d ** -0.5),
        'w2': nrm(ks[22], (DEPTH, N_EXPERTS, EXPERT_FF, d), BETA * EXPERT_FF ** -0.5),
        'ln_g': 1.0 + nrm(ks[23], (DEPTH, 3, d), 0.02),
        'ln_b': nrm(ks[24], (DEPTH, 3, d), 0.02),
    }


def reference(x, mem, ln_in_g, ln_in_b, w_in, fox_fb, lam_q1, lam_k1, lam_q2, lam_k2,
              diff_norm_g, hgrn_lb, hgrn_norm_g, w_out, mem_wq, mem_wk, mem_wv, mem_wo,
              router_w, router_b, w1, w3, w2, ln_g, ln_b):
    lbs = jax.nn.softmax(hgrn_lb.astype(jnp.float32), axis=0)
    lbs = jnp.cumsum(lbs, axis=0) - lbs[0]
    h = _layer_norm(x, ln_in_g, ln_in_b)
    for i in range(DEPTH):
        lam_init = 0.8 - 0.6 * math.exp(-0.3 * i)
        y = _hybrid_mixer(h, w_in[i], fox_fb[i], lam_q1[i], lam_k1[i], lam_q2[i], lam_k2[i],
                          lam_init, diff_norm_g[i], jnp.maximum(lbs[i], 0.0), hgrn_norm_g[i], w_out[i])
        h = _layer_norm(ALPHA * h + y, ln_g[i, 0], ln_b[i, 0])
        y = _memory_attention(h, mem, mem_wq[i], mem_wk[i], mem_wv[i], mem_wo[i])
        h = _layer_norm(ALPHA * h + y, ln_g[i, 1], ln_b[i, 1])
        y = _grouped_moe(h, router_w, router_b, w1[i], w3[i], w2[i])
        h = _layer_norm(ALPHA * h + y, ln_g[i, 2], ln_b[i, 2])
    return h
```

```python
import functools
import math

import numpy as np
import jax
import jax.numpy as jnp
from jax import lax
from jax.experimental import pallas as pl
from jax.experimental.pallas import tpu as pltpu

F32 = jnp.float32
BF16 = jnp.bfloat16

D_MODEL = 1024
DEPTH = 2
FOX_DIM = 64
FOX_HEADS = 4
DIFF_QK_DIM = 32
DIFF_V_DIM = 64
DIFF_HEADS = 4
HGRN_DIM = 128
HGRN_HEADS = 4
HGRN_W = HGRN_DIM * HGRN_HEADS
HGRN_CHUNK = 64
MEM_HEADS = 4
MEM_DIM = D_MODEL // MEM_HEADS
N_EXPERTS = 16
N_GROUPS = 4
EXPERTS_PER_GROUP = 4
EXPERT_FF = D_MODEL // 2
ALPHA = (2 * DEPTH) ** 0.25
LN_EPS = 1e-5
LOG2E = 1.4426950408889634
NEG_INIT = -1e30
LANES = 128
V_ROWS = 80
VMEM_LIMIT = 56 * 1024 * 1024

C_FQ, C_FK, C_FV, C_DQ, C_DK, C_DV, C_HQ, C_HI, C_HG = (0, 256, 512, 768, 1024, 1280, 1536, 2048, 2560)
MAIN_W = 3072
GATE_W = HGRN_W + LANES


def _params(sem, vmem=VMEM_LIMIT):
    return pltpu.CompilerParams(dimension_semantics=sem, vmem_limit_bytes=vmem)


def _split3(x):
    hi = x.astype(BF16)
    r = x - hi.astype(F32)
    mid = r.astype(BF16)
    lo = (r - mid.astype(F32)).astype(BF16)
    return hi, mid, lo


def _dot(a, b):
    return jnp.dot(a, b, preferred_element_type=F32)


def _dot_nt(a, b):
    return lax.dot_general(a, b, (((1,), (1,)), ((), ())), preferred_element_type=F32)


def _sigmoid(x):
    return 1.0 / (1.0 + jnp.exp(-x))


def _log_sigmoid(x):
    return jnp.minimum(x, 0.0) - jnp.log(1.0 + jnp.exp(-jnp.abs(x)))


def _layer_norm(x, g, b):
    mu = jnp.mean(x, axis=-1, keepdims=True)
    xc = x - mu
    var = jnp.mean(xc * xc, axis=-1, keepdims=True)
    return xc * lax.rsqrt(var + LN_EPS) * g + b


def _ln_kernel(x_ref, g_ref, b_ref, h_ref, hb_ref):
    y = _layer_norm(x_ref[...], g_ref[...], b_ref[...])
    h_ref[...] = y
    hb_ref[...] = y.astype(BF16)


def _entry_ln(x, g, b, tm):
    n, d = x.shape
    row = pl.BlockSpec((tm, d), lambda i: (i, 0))
    vec = pl.BlockSpec((1, d), lambda i: (0, 0))
    return pl.pallas_call(
        _ln_kernel, grid=(n // tm,), in_specs=[row, vec, vec], out_specs=[row, row],
        out_shape=[jax.ShapeDtypeStruct((n, d), F32), jax.ShapeDtypeStruct((n, d), BF16)],
        compiler_params=_params(("parallel",)), name="entry_ln")(x, g, b)


def _mm_kernel(x_ref, w_ref, o_ref):
    o_ref[...] = _dot(x_ref[...], w_ref[...]).astype(o_ref.dtype)


def _matmul(x, w, out_dtype, tm, tn, name):
    n, k = x.shape
    m = w.shape[1]
    return pl.pallas_call(
        _mm_kernel, grid=(n // tm, m // tn),
        in_specs=[pl.BlockSpec((tm, k), lambda i, j: (i, 0)), pl.BlockSpec((k, tn), lambda i, j: (0, j))],
        out_specs=pl.BlockSpec((tm, tn), lambda i, j: (i, j)),
        out_shape=jax.ShapeDtypeStruct((n, m), out_dtype),
        compiler_params=_params(("parallel", "arbitrary")), name=name)(x, w)


def _fox_prep_kernel(fq_ref, fk_ref, ff_ref, fb_ref, tri_ref, sel_ref, eq_ref, ek_ref, cq_ref, ck_ref,
                     qa_ref, ka_ref, carry_ref):
    @pl.when(pl.program_id(1) == 0)
    def _():
        carry_ref[...] = jnp.zeros_like(carry_ref)

    logf = _log_sigmoid(ff_ref[...] + fb_ref[...])
    tri = tri_ref[...]
    hi, mid, lo = _split3(logf)
    cum = _dot(tri, hi) + _dot(tri, mid) + _dot(tri, lo) + carry_ref[...]
    ts = cum.shape[0]
    carry_ref[...] = cum[ts - 1:ts, :]
    parts = jnp.concatenate(_split3(cum * LOG2E), axis=1)
    fq = fq_ref[...]
    fk = fk_ref[...]
    for h in range(FOX_HEADS):
        pair = slice((h // 2) * LANES, (h // 2 + 1) * LANES)
        sel = sel_ref[h % 2]
        qa = _dot(fq[:, pair], sel) + _dot(parts, eq_ref[h]) + cq_ref[...]
        ka = _dot(fk[:, pair], sel) + _dot(parts, ek_ref[h]) + ck_ref[...]
        qa_ref[0, h] = qa.astype(BF16)
        ka_ref[0, h] = ka.astype(BF16)


def _fox_prep_consts():
    sel = np.zeros((2, LANES, LANES), np.float32)
    for half in range(2):
        for d in range(FOX_DIM):
            sel[half, half * FOX_DIM + d, d] = 1.0
    eq = np.zeros((FOX_HEADS, 3 * LANES, LANES), np.float32)
    ek = np.zeros((FOX_HEADS, 3 * LANES, LANES), np.float32)
    for h in range(FOX_HEADS):
        for p in range(3):
            eq[h, p * LANES + h, FOX_DIM + p] = 1.0
            ek[h, p * LANES + h, FOX_DIM + 3 + p] = -1.0
    cq = np.zeros((1, LANES), np.float32)
    ck = np.zeros((1, LANES), np.float32)
    cq[0, FOX_DIM + 3:FOX_DIM + 6] = 1.0
    ck[0, FOX_DIM:FOX_DIM + 3] = 1.0
    return (jnp.asarray(sel, BF16), jnp.asarray(eq, BF16), jnp.asarray(ek, BF16), jnp.asarray(cq), jnp.asarray(ck))


def _fox_prep(p_main, g_gate, fb, batch, seq, ts):
    ns = seq // ts
    tri = jnp.asarray(np.tril(np.ones((ts, ts), np.float32)), BF16)
    sel, eq, ek, cq, ck = _fox_prep_consts()
    fb_pad = jnp.zeros((1, LANES), F32).at[0, :FOX_HEADS].set(fb)
    const2 = lambda shape: pl.BlockSpec(shape, lambda b, s: (0,) * len(shape))
    out_spec = pl.BlockSpec((1, FOX_HEADS, ts, LANES), lambda b, s: (b, 0, s, 0))
    out_sds = jax.ShapeDtypeStruct((batch, FOX_HEADS, seq, LANES), BF16)
    return pl.pallas_call(
        _fox_prep_kernel, grid=(batch, ns),
        in_specs=[pl.BlockSpec((ts, 256), lambda b, s: (b * ns + s, C_FQ // 256)),
                  pl.BlockSpec((ts, 256), lambda b, s: (b * ns + s, C_FK // 256)),
                  pl.BlockSpec((ts, LANES), lambda b, s: (b * ns + s, HGRN_W // LANES)),
                  const2((1, LANES)), const2((ts, ts)), const2((2, LANES, LANES)),
                  const2((FOX_HEADS, 3 * LANES, LANES)), const2((FOX_HEADS, 3 * LANES, LANES)),
                  const2((1, LANES)), const2((1, LANES))],
        out_specs=[out_spec, out_spec], out_shape=[out_sds, out_sds],
        scratch_shapes=[pltpu.VMEM((1, LANES), F32)],
        compiler_params=_params(("arbitrary", "arbitrary")), name="fox_prep",
    )(p_main, p_main, g_gate, fb_pad, tri, sel, eq, ek, cq, ck)


def _softmax_step(s, m, acc, vt):
    m_new = jnp.maximum(m, jnp.max(s, axis=0, keepdims=True))
    alpha = jnp.exp2(m - m_new)
    p = jnp.exp2(s - m_new).astype(BF16)
    return m_new, alpha * acc + _dot(vt, p)


def _finish(acc, dim):
    return acc[:dim] / acc[dim:dim + 1]


def _store_transposed(o_ref, h, o, tile):
    dim = o.shape[0]
    padded = jnp.concatenate([o, jnp.zeros((LANES - dim, tile), F32)], axis=0)
    o_ref[0, h] = padded.T[:, :dim].astype(o_ref.dtype)


def _fox_attn_kernel(q_ref, k_ref, vt_ref, o_ref, *, tile):
    qi = pl.program_id(1)
    qs = [q_ref[0, h] for h in range(FOX_HEADS)]
    row = lax.broadcasted_iota(jnp.int32, (tile, tile), 0)
    col = lax.broadcasted_iota(jnp.int32, (tile, tile), 1)
    visible = row <= col

    def step(j, carry, masked):
        off = pl.multiple_of(j * tile, tile)
        out = []
        for h in range(FOX_HEADS):
            m, acc = carry[h]
            s = _dot_nt(k_ref[0, h, pl.ds(off, tile), :], qs[h])
            if masked:
                s = jnp.where(visible, s, -jnp.inf)
            out.append(_softmax_step(s, m, acc, vt_ref[0, h, :, pl.ds(off, tile)]))
        return tuple(out)

    init = tuple((jnp.full((1, tile), NEG_INIT, F32), jnp.zeros((V_ROWS, tile), F32)) for _ in range(FOX_HEADS))
    carry = lax.fori_loop(0, qi, lambda j, c: step(j, c, False), init)
    carry = step(qi, carry, True)
    for h in range(FOX_HEADS):
        _store_transposed(o_ref, h, _finish(carry[h][1], FOX_DIM), tile)


def _fox_attn(q_aug, k_aug, vt_aug, tile):
    batch, heads, seq, _ = q_aug.shape
    return pl.pallas_call(
        functools.partial(_fox_attn_kernel, tile=tile), grid=(batch, seq // tile),
        in_specs=[pl.BlockSpec((1, heads, tile, LANES), lambda b, i: (b, 0, i, 0)),
                  pl.BlockSpec((1, heads, seq, LANES), lambda b, i: (b, 0, 0, 0)),
                  pl.BlockSpec((1, heads, V_ROWS, seq), lambda b, i: (b, 0, 0, 0))],
        out_specs=pl.BlockSpec((1, heads, tile, FOX_DIM), lambda b, i: (b, 0, i, 0)),
        out_shape=jax.ShapeDtypeStruct((batch, heads, seq, FOX_DIM), BF16),
        compiler_params=_params(("parallel", "arbitrary")), name="fox_attn")(q_aug, k_aug, vt_aug)


def _diff_attn_kernel(q_ref, k_ref, vt_ref, lam_ref, g_ref, o_ref, *, tile, lam_init):
    qi = pl.program_id(1)
    lane = lax.broadcasted_iota(jnp.int32, (tile, DIFF_V_DIM), 1)
    q1s, q2s = [], []
    for h in range(DIFF_HEADS):
        q = q_ref[0, h]
        q1s.append(jnp.where(lane < DIFF_QK_DIM, q, jnp.zeros_like(q)))
        q2s.append(jnp.where(lane >= DIFF_QK_DIM, q, jnp.zeros_like(q)))
    row = lax.broadcasted_iota(jnp.int32, (tile, tile), 0)
    col = lax.broadcasted_iota(jnp.int32, (tile, tile), 1)
    visible = (row // HGRN_CHUNK) <= (col // HGRN_CHUNK)

    def step(j, carry, masked):
        off = pl.multiple_of(j * tile, tile)
        out = []
        for h in range(DIFF_HEADS):
            m1, a1, m2, a2 = carry[h]
            k = k_ref[0, h, pl.ds(off, tile), :]
            vt = vt_ref[0, h, :, pl.ds(off, tile)]
            s1 = _dot_nt(k, q1s[h])
            s2 = _dot_nt(k, q2s[h])
            if masked:
                s1 = jnp.where(visible, s1, -jnp.inf)
                s2 = jnp.where(visible, s2, -jnp.inf)
            m1, a1 = _softmax_step(s1, m1, a1, vt)
            m2, a2 = _softmax_step(s2, m2, a2, vt)
            out.append((m1, a1, m2, a2))
        return tuple(out)

    def fresh():
        return jnp.full((1, tile), NEG_INIT, F32), jnp.zeros((V_ROWS, tile), F32)

    init = tuple(fresh() + fresh() for _ in range(DIFF_HEADS))
    carry = lax.fori_loop(0, qi, lambda j, c: step(j, c, False), init)
    carry = step(qi, carry, True)

    lp = lam_ref[...]
    lam = (jnp.exp(jnp.sum(lp[0:1] * lp[1:2], axis=1, keepdims=True))
           - jnp.exp(jnp.sum(lp[2:3] * lp[3:4], axis=1, keepdims=True)) + lam_init)
    for h in range(DIFF_HEADS):
        _, a1, _, a2 = carry[h]
        o = _finish(a1, DIFF_V_DIM) - lam * _finish(a2, DIFF_V_DIM)
        o = o * lax.rsqrt(jnp.mean(o * o, axis=0, keepdims=True) + LN_EPS)
        dim = o.shape[0]
        padded = jnp.concatenate([o, jnp.zeros((LANES - dim, tile), F32)], axis=0)
        y = padded.T[:, :dim] * g_ref[...] * (1.0 - lam_init)
        o_ref[0, h] = y.astype(o_ref.dtype)


def _diff_attn(dq, dk, dvt_aug, lam_pack, norm_g, lam_init, tile):
    batch, heads, seq, qk2 = dq.shape
    return pl.pallas_call(
        functools.partial(_diff_attn_kernel, tile=tile, lam_init=lam_init), grid=(batch, seq // tile),
        in_specs=[pl.BlockSpec((1, heads, tile, qk2), lambda b, i: (b, 0, i, 0)),
                  pl.BlockSpec((1, heads, seq, qk2), lambda b, i: (b, 0, 0, 0)),
                  pl.BlockSpec((1, heads, V_ROWS, seq), lambda b, i: (b, 0, 0, 0)),
                  pl.BlockSpec((8, LANES), lambda b, i: (0, 0)),
                  pl.BlockSpec((1, DIFF_V_DIM), lambda b, i: (0, 0))],
        out_specs=pl.BlockSpec((1, heads, tile, DIFF_V_DIM), lambda b, i: (b, 0, i, 0)),
        out_shape=jax.ShapeDtypeStruct((batch, heads, seq, DIFF_V_DIM), BF16),
        compiler_params=_params(("parallel", "arbitrary")), name="diff_attn")(dq, dk, dvt_aug, lam_pack, norm_g)


def _hgrn_kernel(hq_ref, hi_ref, hg_ref, hf_ref, lb_ref, g_ref, tri_ref, o_ref, state_ref, *, layer, tc):
    @pl.when(pl.program_id(1) == 0)
    def _():
        state_ref[...] = jnp.zeros_like(state_ref)

    lb_all = lb_ref[...]
    e = jnp.exp(lb_all - jnp.max(lb_all, axis=0, keepdims=True))
    prob = e / jnp.sum(e, axis=0, keepdims=True)
    lb_row = jnp.zeros((1, HGRN_W), F32)
    for j in range(1, layer + 1):
        lb_row = lb_row + prob[j:j + 1]
    lb_row = jnp.maximum(lb_row, 0.0)

    tri = tri_ref[...]
    c = HGRN_CHUNK
    r_i = lax.broadcasted_iota(jnp.int32, (c, c), 0)
    c_i = lax.broadcasted_iota(jnp.int32, (c, c), 1)
    causal = r_i >= c_i
    half = c // 2 - 1

    for h in range(HGRN_HEADS):
        hs = slice(h * HGRN_DIM, (h + 1) * HGRN_DIM)
        lb = lb_row[:, hs]
        log_lb = jnp.log(lb)
        log_1m = jnp.log(1.0 - lb)
        z = hf_ref[:, hs]
        b_term = log_1m + _log_sigmoid(z)
        logf = jnp.maximum(log_lb, b_term) + jnp.log(1.0 + jnp.exp(-jnp.abs(log_lb - b_term)))
        kk = (1.0 - lb) * _sigmoid(-z)
        xq = hq_ref[:, hs].astype(F32)
        q = xq * _sigmoid(xq)
        v = hi_ref[:, hs]
        xg = hg_ref[:, hs].astype(F32)
        gate = xg * _sigmoid(xg)
        st = state_ref[h]
        for ci in range(tc // c):
            rs = slice(ci * c, (ci + 1) * c)
            g_hi, g_mid, g_lo = _split3(logf[rs])
            bc = _dot(tri, g_hi) + _dot(tri, g_mid) + _dot(tri, g_lo)
            ref_row = bc[half:half + 1]
            b_last = bc[c - 1:c]
            qc, kc, vc = q[rs], kk[rs], v[rs]
            a = _dot_nt((qc * jnp.exp(bc - ref_row)).astype(BF16), (kc * jnp.exp(ref_row - bc)).astype(BF16))
            a = jnp.where(causal, a, 0.0)
            o = _dot(a.astype(BF16), vc) + _dot_nt((qc * jnp.exp(bc)).astype(BF16), st.astype(BF16))
            kd = (kc * jnp.exp(b_last - bc)).astype(BF16)
            vt = vc.astype(F32).T.astype(BF16)
            st = st * jnp.exp(b_last) + _dot(vt, kd)
            y = o * lax.rsqrt(jnp.mean(o * o, axis=1, keepdims=True) + LN_EPS) * g_ref[...] * gate[rs]
            o_ref[rs, hs] = y.astype(o_ref.dtype)
        state_ref[h] = st


def _hgrn(p_main, g_gate, hgrn_lb, norm_g, layer, batch, seq, tc):
    ns = seq // tc
    tri = jnp.asarray(np.tril(np.ones((HGRN_CHUNK, HGRN_CHUNK), np.float32)), BF16)
    col = lambda c0: pl.BlockSpec((tc, HGRN_W), lambda b, s: (b * ns + s, c0 // HGRN_W))
    return pl.pallas_call(
        functools.partial(_hgrn_kernel, layer=layer, tc=tc), grid=(batch, ns),
        in_specs=[col(C_HQ), col(C_HI), col(C_HG),
                  pl.BlockSpec((tc, HGRN_W), lambda b, s: (b * ns + s, 0)),
                  pl.BlockSpec((DEPTH, HGRN_W), lambda b, s: (0, 0)),
                  pl.BlockSpec((1, HGRN_DIM), lambda b, s: (0, 0)),
                  pl.BlockSpec((HGRN_CHUNK, HGRN_CHUNK), lambda b, s: (0, 0))],
        out_specs=pl.BlockSpec((tc, HGRN_W), lambda b, s: (b * ns + s, 0)),
        out_shape=jax.ShapeDtypeStruct((batch * seq, HGRN_W), BF16),
        scratch_shapes=[pltpu.VMEM((HGRN_HEADS, HGRN_DIM, HGRN_DIM), F32)],
        compiler_params=_params(("parallel", "arbitrary")), name="hgrn2",
    )(p_main, p_main, p_main, g_gate, hgrn_lb, norm_g, tri)


def _proj_ln_kernel(y_ref, w_ref, h_ref, g_ref, b_ref, ho_ref, hbo_ref):
    z = ALPHA * h_ref[...] + _dot(y_ref[...], w_ref[...])
    out = _layer_norm(z, g_ref[...], b_ref[...])
    ho_ref[...] = out
    hbo_ref[...] = out.astype(BF16)


def _proj_ln(y, w, h, g, b, tm):
    n, d = h.shape
    row = lambda width: pl.BlockSpec((tm, width), lambda i: (i, 0))
    vec = pl.BlockSpec((1, d), lambda i: (0, 0))
    return pl.pallas_call(
        _proj_ln_kernel, grid=(n // tm,),
        in_specs=[row(y.shape[1]), pl.BlockSpec(w.shape, lambda i: (0, 0)), row(d), vec, vec],
        out_specs=[row(d), row(d)],
        out_shape=[jax.ShapeDtypeStruct((n, d), F32), jax.ShapeDtypeStruct((n, d), BF16)],
        compiler_params=_params(("parallel",)), name="out_proj_ln")(y, w, h, g, b)


def _mem_attn_kernel(hb_ref, h_ref, wq_ref, k_ref, v_ref, wo_ref, g_ref, b_ref, ho_ref, hbo_ref):
    q = _dot(hb_ref[...], wq_ref[...])
    outs = []
    for hd in range(MEM_HEADS):
        hs = slice(hd * MEM_DIM, (hd + 1) * MEM_DIM)
        s = _dot_nt(q[:, hs].astype(BF16), k_ref[:, hs])
        p = jnp.exp2(s - jnp.max(s, axis=1, keepdims=True))
        o = _dot(p.astype(BF16), v_ref[:, hs]) / jnp.sum(p, axis=1, keepdims=True)
        outs.append(o.astype(BF16))
    att = jnp.concatenate(outs, axis=1)
    z = ALPHA * h_ref[...] + _dot(att, wo_ref[...])
    out = _layer_norm(z, g_ref[...], b_ref[...])
    ho_ref[...] = out
    hbo_ref[...] = out.astype(BF16)


def _mem_attn(hb, h, wq, kmem, vmem, wo, g, b, batch, seq, tm):
    n, d = h.shape
    ns = seq // tm
    mem_len = kmem.shape[0] // batch
    row = pl.BlockSpec((tm, d), lambda bi, s: (bi * ns + s, 0))
    full = pl.BlockSpec((d, d), lambda bi, s: (0, 0))
    mem = pl.BlockSpec((mem_len, d), lambda bi, s: (bi, 0))
    vec = pl.BlockSpec((1, d), lambda bi, s: (0, 0))
    return pl.pallas_call(
        _mem_attn_kernel, grid=(batch, ns),
        in_specs=[row, row, full, mem, mem, full, vec, vec], out_specs=[row, row],
        out_shape=[jax.ShapeDtypeStruct((n, d), F32), jax.ShapeDtypeStruct((n, d), BF16)],
        compiler_params=_params(("parallel", "arbitrary")), name="mem_attn")(hb, h, wq, kmem, vmem, wo, g, b)


def _top2_sum(a, b, c, d):
    hi1, lo1 = jnp.maximum(a, b), jnp.minimum(a, b)
    hi2, lo2 = jnp.maximum(c, d), jnp.minimum(c, d)
    return jnp.maximum(hi1, hi2) + jnp.maximum(jnp.minimum(hi1, hi2), jnp.maximum(lo1, lo2))


def _router_kernel(h_ref, rw_ref, rb_ref, gd_ref, rows_ref):
    xh, xm, xl = _split3(h_ref[...])
    wh, wm, wl = rw_ref[0], rw_ref[1], rw_ref[2]
    logits = (_dot_nt(wh, xh) + _dot_nt(wh, xm) + _dot_nt(wm, xh)
              + _dot_nt(wh, xl) + _dot_nt(wm, xm) + _dot_nt(wl, xh))
    scores = _sigmoid(logits)
    biased = scores + rb_ref[...]
    s_rows = [scores[e:e + 1] for e in range(N_EXPERTS)]
    b_rows = [biased[e:e + 1] for e in range(N_EXPERTS)]
    group_score = [_top2_sum(*b_rows[4 * g:4 * g + 4]) for g in range(N_GROUPS)]
    best = group_score[0]
    best_idx = jnp.zeros_like(best, dtype=jnp.int32)
    for g in range(1, N_GROUPS):
        better = group_score[g] > best
        best_idx = jnp.where(better, g, best_idx)
        best = jnp.where(better, group_score[g], best)
    picked = []
    for e in range(N_EXPERTS):
        g = e // EXPERTS_PER_GROUP
        rank = jnp.zeros_like(best)
        for j in range(4 * g, 4 * g + 4):
            if j == e:
                continue
            ahead = (b_rows[j] > b_rows[e]) | (b_rows[j] == b_rows[e]) if j < e else (b_rows[j] > b_rows[e])
            rank = rank + jnp.where(ahead, 1.0, 0.0)
        picked.append((best_idx == g) & (rank < 2.0))
    denom = jnp.zeros_like(best)
    for e in range(N_EXPERTS):
        denom = denom + jnp.where(picked[e], s_rows[e], 0.0)
    rows_ref[...] = jnp.zeros_like(rows_ref)
    for e in range(N_EXPERTS):
        rows_ref[e:e + 1, :] = jnp.where(picked[e], s_rows[e] / denom, 0.0)
    gd_ref[...] = rows_ref[...].T


def _router(h, rw_parts, rb_col, tm):
    n, d = h.shape
    return pl.pallas_call(
        _router_kernel, grid=(n // tm,),
        in_specs=[pl.BlockSpec((tm, d), lambda i: (i, 0)),
                  pl.BlockSpec((3, N_EXPERTS, d), lambda i: (0, 0, 0)),
                  pl.BlockSpec((N_EXPERTS, 1), lambda i: (0, 0))],
        out_specs=pl.BlockSpec((tm, LANES), lambda i: (i, 0)),
        out_shape=jax.ShapeDtypeStruct((n, LANES), F32),
        scratch_shapes=[pltpu.VMEM((LANES, tm), F32)],
        compiler_params=_params(("parallel",)), name="router")(h, rw_parts, rb_col)


def _moe_kernel(hb_ref, h_ref, gd_ref, w1_ref, w3_ref, w2_ref, g_ref, b_ref, ho_ref, hbo_ref, acc_ref):
    e = pl.program_id(1)

    @pl.when(e == 0)
    def _():
        acc_ref[...] = jnp.zeros_like(acc_ref)

    x = hb_ref[...]
    a = _dot(x, w1_ref[0])
    hid = (a * _sigmoid(a)) * _dot(x, w3_ref[0])
    y = _dot(hid.astype(BF16), w2_ref[0])
    gd = gd_ref[...]
    lane = lax.broadcasted_iota(jnp.int32, gd.shape, 1)
    gate = jnp.sum(jnp.where(lane == e, gd, 0.0), axis=1, keepdims=True)
    acc_ref[...] += gate * y

    @pl.when(e == N_EXPERTS - 1)
    def _():
        out = _layer_norm(ALPHA * h_ref[...] + acc_ref[...], g_ref[...], b_ref[...])
        ho_ref[...] = out
        hbo_ref[...] = out.astype(BF16)


def _moe(hb, h, gd, w1, w3, w2, g, b, tm):
    n, d = h.shape
    ff = w1.shape[2]
    row = lambda width: pl.BlockSpec((tm, width), lambda i, e: (i, 0))
    vec = pl.BlockSpec((1, d), lambda i, e: (0, 0))
    return pl.pallas_call(
        _moe_kernel, grid=(n // tm, N_EXPERTS),
        in_specs=[row(d), row(d), row(LANES),
                  pl.BlockSpec((1, d, ff), lambda i, e: (e, 0, 0)),
                  pl.BlockSpec((1, d, ff), lambda i, e: (e, 0, 0)),
                  pl.BlockSpec((1, ff, d), lambda i, e: (e, 0, 0)), vec, vec],
        out_specs=[row(d), row(d)],
        out_shape=[jax.ShapeDtypeStruct((n, d), F32), jax.ShapeDtypeStruct((n, d), BF16)],
        scratch_shapes=[pltpu.VMEM((tm, d), F32)],
        compiler_params=_params(("parallel", "arbitrary")), name="moe")(hb, h, gd, w1, w3, w2, g, b)


def _pack_in_weights(w_in):
    sizes = (256, 256, 256, FOX_HEADS, 256, 256, 256, HGRN_W, HGRN_W, HGRN_W, HGRN_W)
    offs = np.cumsum((0,) + sizes)
    fq, fk, fv, ff, dq, dk, dv, hq, hf, hi, hg = (w_in[:, offs[i]:offs[i + 1]] for i in range(len(sizes)))
    fox_scale = FOX_DIM ** -0.5 * LOG2E
    diff_scale = DIFF_QK_DIM ** -0.5 * LOG2E
    main = jnp.concatenate([fq * fox_scale, fk, fv, dq * diff_scale, dk, dv, hq, hi, hg], axis=1).astype(BF16)
    ff_pad = jnp.zeros((w_in.shape[0], LANES), w_in.dtype).at[:, :FOX_HEADS].set(ff)
    gate = jnp.concatenate([hf, ff_pad], axis=1).astype(BF16)
    return main, gate


def _heads(p, c0, batch, seq, heads, dim):
    return p[:, c0:c0 + heads * dim].reshape(batch, seq, heads, dim).transpose(0, 2, 1, 3)


def _value_rows(p, c0, batch, seq, heads, dim):
    vt = p[:, c0:c0 + heads * dim].reshape(batch, seq, heads, dim).transpose(0, 2, 3, 1)
    ones = jnp.ones((batch, heads, 1, seq), p.dtype)
    pad = jnp.zeros((batch, heads, V_ROWS - dim - 1, seq), p.dtype)
    return jnp.concatenate([vt, ones, pad], axis=2)


def _merge(o):
    b, h, s, d = o.shape
    return o.transpose(0, 2, 1, 3).reshape(b * s, h * d)


def kernel(x, mem, ln_in_g, ln_in_b, w_in, fox_fb, lam_q1, lam_k1, lam_q2, lam_k2, diff_norm_g, hgrn_lb,
           hgrn_norm_g, w_out, mem_wq, mem_wk, mem_wv, mem_wo, router_w, router_b, w1, w3, w2, ln_g, ln_b):
    batch, seq, d = x.shape
    n = batch * seq
    tm = min(512, seq)
    attn_tile = min(512, seq)
    vec = lambda v: v.reshape(1, -1).astype(F32)

    h, hb = _entry_ln(x.reshape(n, d), vec(ln_in_g), vec(ln_in_b), tm)
    memb = mem.reshape(-1, d).astype(BF16)
    rw_parts = jnp.stack(_split3(router_w.T.astype(F32)))
    rb_col = router_b.reshape(N_EXPERTS, 1).astype(F32)
    hgrn_lb = hgrn_lb.astype(F32)

    for i in range(DEPTH):
        lam_init = 0.8 - 0.6 * math.exp(-0.3 * i)
        w_main, w_gate = _pack_in_weights(w_in[i])
        p_main = _matmul(hb, w_main, BF16, tm, 512, "in_proj")
        g_gate = _matmul(hb, w_gate, F32, tm, GATE_W, "gate_proj")

        q_aug, k_aug = _fox_prep(p_main, g_gate, fox_fb[i].astype(F32), batch, seq, min(512, seq))
        o_fox = _fox_attn(q_aug, k_aug, _value_rows(p_main, C_FV, batch, seq, FOX_HEADS, FOX_DIM), attn_tile)

        lam_pack = jnp.zeros((8, LANES), F32)
        for r, v in enumerate((lam_q1[i], lam_k1[i], lam_q2[i], lam_k2[i])):
            lam_pack = lam_pack.at[r, :DIFF_QK_DIM].set(v.astype(F32))
        o_diff = _diff_attn(_heads(p_main, C_DQ, batch, seq, DIFF_HEADS, DIFF_V_DIM),
                            _heads(p_main, C_DK, batch, seq, DIFF_HEADS, DIFF_V_DIM),
                            _value_rows(p_main, C_DV, batch, seq, DIFF_HEADS, DIFF_V_DIM),
                            lam_pack, vec(diff_norm_g[i]), lam_init, attn_tile)

        y_hgrn = _hgrn(p_main, g_gate, hgrn_lb, vec(hgrn_norm_g[i]), i, batch, seq, min(256, seq))
        y = jnp.concatenate([_merge(o_fox), _merge(o_diff), y_hgrn], axis=1)
        h, hb = _proj_ln(y, w_out[i].astype(BF16), h, vec(ln_g[i, 0]), vec(ln_b[i, 0]), tm)

        kmem = _matmul(memb, mem_wk[i].astype(BF16), BF16, memb.shape[0], 512, "mem_k")
        vmem = _matmul(memb, mem_wv[i].astype(BF16), BF16, memb.shape[0], 512, "mem_v")
        wq = (mem_wq[i] * (MEM_DIM ** -0.5 * LOG2E)).astype(BF16)
        h, hb = _mem_attn(hb, h, wq, kmem, vmem, mem_wo[i].astype(BF16), vec(ln_g[i, 1]), vec(ln_b[i, 1]),
                          batch, seq, tm)

        gd = _router(h, rw_parts, rb_col, tm)
        h, hb = _moe(hb, h, gd, w1[i].astype(BF16), w3[i].astype(BF16), w2[i].astype(BF16),
                     vec(ln_g[i, 2]), vec(ln_b[i, 2]), min(1024, n))
    return h.reshape(batch, seq, d)
```

```python
import functools
import math

import numpy as np
import jax
import jax.numpy as jnp
from jax import lax
from jax.experimental import pallas as pl
from jax.experimental.pallas import tpu as pltpu

F32 = jnp.float32
BF16 = jnp.bfloat16

D_MODEL = 1024
DEPTH = 2
FOX_DIM = 64
FOX_HEADS = 4
DIFF_QK_DIM = 32
DIFF_V_DIM = 64
DIFF_HEADS = 4
HGRN_DIM = 128
HGRN_HEADS = 4
HGRN_W = HGRN_DIM * HGRN_HEADS
HGRN_CHUNK = 64
MEM_HEADS = 4
MEM_DIM = D_MODEL // MEM_HEADS
N_EXPERTS = 16
N_GROUPS = 4
EXPERTS_PER_GROUP = 4
EXPERT_FF = D_MODEL // 2
ALPHA = (2 * DEPTH) ** 0.25
LN_EPS = 1e-5
LOG2E = 1.4426950408889634
NEG_INIT = -1e30
LANES = 128
V_ROWS = 80
VMEM_LIMIT = 56 * 1024 * 1024

C_FQ, C_FK, C_FV, C_DQ, C_DK, C_DV, C_HQ, C_HI, C_HG = (0, 256, 512, 768, 1024, 1280, 1536, 2048, 2560)
MAIN_W = 3072
GATE_W = HGRN_W + LANES


def _params(sem, vmem=VMEM_LIMIT, flags=None):
    return pltpu.CompilerParams(dimension_semantics=sem, vmem_limit_bytes=vmem, flags=flags)


def _split3(x):
    hi = x.astype(BF16)
    r = x - hi.astype(F32)
    mid = r.astype(BF16)
    lo = (r - mid.astype(F32)).astype(BF16)
    return hi, mid, lo


def _dot(a, b):
    return jnp.dot(a, b, preferred_element_type=F32)


def _dot_nt(a, b):
    return lax.dot_general(a, b, (((1,), (1,)), ((), ())), preferred_element_type=F32)


def _sigmoid(x):
    return 1.0 / (1.0 + jnp.exp(-x))


def _log_sigmoid(x):
    return jnp.minimum(x, 0.0) - jnp.log(1.0 + jnp.exp(-jnp.abs(x)))


def _layer_norm(x, g, b):
    mu = jnp.mean(x, axis=-1, keepdims=True)
    xc = x - mu
    var = jnp.mean(xc * xc, axis=-1, keepdims=True)
    return xc * lax.rsqrt(var + LN_EPS) * g + b


def _ln_kernel(x_ref, g_ref, b_ref, h_ref, hb_ref):
    y = _layer_norm(x_ref[...], g_ref[...], b_ref[...])
    h_ref[...] = y
    hb_ref[...] = y.astype(BF16)


def _entry_ln(x, g, b, tm):
    n, d = x.shape
    row = pl.BlockSpec((tm, d), lambda i: (i, 0))
    vec = pl.BlockSpec((1, d), lambda i: (0, 0))
    return pl.pallas_call(
        _ln_kernel, grid=(n // tm,), in_specs=[row, vec, vec], out_specs=[row, row],
        out_shape=[jax.ShapeDtypeStruct((n, d), F32), jax.ShapeDtypeStruct((n, d), BF16)],
        compiler_params=_params(("parallel",)), name="entry_ln")(x, g, b)


def _mm_kernel(x_ref, w_ref, o_ref):
    o_ref[...] = _dot(x_ref[...], w_ref[...]).astype(o_ref.dtype)


def _matmul(x, w, out_dtype, tm, tn, name):
    n, k = x.shape
    m = w.shape[1]
    return pl.pallas_call(
        _mm_kernel, grid=(n // tm, m // tn),
        in_specs=[pl.BlockSpec((tm, k), lambda i, j: (i, 0)), pl.BlockSpec((k, tn), lambda i, j: (0, j))],
        out_specs=pl.BlockSpec((tm, tn), lambda i, j: (i, j)),
        out_shape=jax.ShapeDtypeStruct((n, m), out_dtype),
        compiler_params=_params(("parallel", "arbitrary")), name=name)(x, w)


IN_PROJ_CHUNK = 512


def _in_proj_kernel(x_ref, wm_ref, wg_ref, pm_ref, pg_ref):
    x = x_ref[...]
    for c0 in range(0, MAIN_W, IN_PROJ_CHUNK):
        cs = slice(c0, c0 + IN_PROJ_CHUNK)
        pm_ref[:, cs] = _dot(x, wm_ref[:, cs]).astype(BF16)
    pg_ref[...] = _dot(x, wg_ref[...])


def _in_proj(hb, w_main, w_gate, tm):
    n, d = hb.shape
    return pl.pallas_call(
        _in_proj_kernel, grid=(n // tm,),
        in_specs=[pl.BlockSpec((tm, d), lambda i: (i, 0)),
                  pl.BlockSpec((d, MAIN_W), lambda i: (0, 0)), pl.BlockSpec((d, GATE_W), lambda i: (0, 0))],
        out_specs=[pl.BlockSpec((tm, MAIN_W), lambda i: (i, 0)), pl.BlockSpec((tm, GATE_W), lambda i: (i, 0))],
        out_shape=[jax.ShapeDtypeStruct((n, MAIN_W), BF16), jax.ShapeDtypeStruct((n, GATE_W), F32)],
        compiler_params=_params(("parallel",)), name="in_proj")(hb, w_main, w_gate)


def _value_rows(v_ref, vt_ref, heads, dim):
    ts = v_ref.shape[0]
    vt = v_ref[...].astype(F32).T.astype(BF16)
    tail_row = lax.broadcasted_iota(jnp.int32, (V_ROWS - dim, ts), 0)
    tail = jnp.where(tail_row == 0, 1.0, 0.0).astype(BF16)
    for h in range(heads):
        vt_ref[0, h, 0:dim, :] = vt[h * dim:(h + 1) * dim]
        vt_ref[0, h, dim:V_ROWS, :] = tail


def _fox_prep_kernel(fq_ref, fk_ref, fv_ref, dv_ref, ff_ref, fb_ref, tri_ref, sel_ref, eq_ref, ek_ref, cq_ref,
                     ck_ref, qa_ref, ka_ref, fvt_ref, dvt_ref, carry_ref):
    @pl.when(pl.program_id(1) == 0)
    def _():
        carry_ref[...] = jnp.zeros_like(carry_ref)

    _value_rows(fv_ref, fvt_ref, FOX_HEADS, FOX_DIM)
    _value_rows(dv_ref, dvt_ref, DIFF_HEADS, DIFF_V_DIM)

    logf = _log_sigmoid(ff_ref[...] + fb_ref[...])
    tri = tri_ref[...]
    hi, mid, lo = _split3(logf)
    cum = _dot(tri, hi) + _dot(tri, mid) + _dot(tri, lo) + carry_ref[...]
    ts = cum.shape[0]
    carry_ref[...] = cum[ts - 1:ts, :]
    parts = jnp.concatenate(_split3(cum * LOG2E), axis=1)
    fq = fq_ref[...]
    fk = fk_ref[...]
    for h in range(FOX_HEADS):
        pair = slice((h // 2) * LANES, (h // 2 + 1) * LANES)
        sel = sel_ref[h % 2]
        qa = _dot(fq[:, pair], sel) + _dot(parts, eq_ref[h]) + cq_ref[...]
        ka = _dot(fk[:, pair], sel) + _dot(parts, ek_ref[h]) + ck_ref[...]
        qa_ref[0, h] = qa.astype(BF16)
        ka_ref[0, h] = ka.astype(BF16)


def _fox_prep_consts():
    sel = np.zeros((2, LANES, LANES), np.float32)
    for half in range(2):
        for d in range(FOX_DIM):
            sel[half, half * FOX_DIM + d, d] = 1.0
    eq = np.zeros((FOX_HEADS, 3 * LANES, LANES), np.float32)
    ek = np.zeros((FOX_HEADS, 3 * LANES, LANES), np.float32)
    for h in range(FOX_HEADS):
        for p in range(3):
            eq[h, p * LANES + h, FOX_DIM + p] = 1.0
            ek[h, p * LANES + h, FOX_DIM + 3 + p] = -1.0
    cq = np.zeros((1, LANES), np.float32)
    ck = np.zeros((1, LANES), np.float32)
    cq[0, FOX_DIM + 3:FOX_DIM + 6] = 1.0
    ck[0, FOX_DIM:FOX_DIM + 3] = 1.0
    return (jnp.asarray(sel, BF16), jnp.asarray(eq, BF16), jnp.asarray(ek, BF16), jnp.asarray(cq), jnp.asarray(ck))


def _fox_prep(p_main, g_gate, fb, batch, seq, ts):
    ns = seq // ts
    tri = jnp.asarray(np.tril(np.ones((ts, ts), np.float32)), BF16)
    sel, eq, ek, cq, ck = _fox_prep_consts()
    fb_pad = jnp.zeros((1, LANES), F32).at[0, :FOX_HEADS].set(fb)
    const2 = lambda shape: pl.BlockSpec(shape, lambda b, s: (0,) * len(shape))
    out_spec = pl.BlockSpec((1, FOX_HEADS, ts, LANES), lambda b, s: (b, 0, s, 0))
    out_sds = jax.ShapeDtypeStruct((batch, FOX_HEADS, seq, LANES), BF16)
    vt_spec = pl.BlockSpec((1, FOX_HEADS, V_ROWS, ts), lambda b, s: (b, 0, 0, s))
    vt_sds = jax.ShapeDtypeStruct((batch, FOX_HEADS, V_ROWS, seq), BF16)
    col = lambda c0: pl.BlockSpec((ts, 256), lambda b, s: (b * ns + s, c0 // 256))
    return pl.pallas_call(
        _fox_prep_kernel, grid=(batch, ns),
        in_specs=[col(C_FQ), col(C_FK), col(C_FV), col(C_DV),
                  pl.BlockSpec((ts, LANES), lambda b, s: (b * ns + s, HGRN_W // LANES)),
                  const2((1, LANES)), const2((ts, ts)), const2((2, LANES, LANES)),
                  const2((FOX_HEADS, 3 * LANES, LANES)), const2((FOX_HEADS, 3 * LANES, LANES)),
                  const2((1, LANES)), const2((1, LANES))],
        out_specs=[out_spec, out_spec, vt_spec, vt_spec], out_shape=[out_sds, out_sds, vt_sds, vt_sds],
        scratch_shapes=[pltpu.VMEM((1, LANES), F32)],
        compiler_params=_params(("arbitrary", "arbitrary")), name="fox_prep",
    )(p_main, p_main, p_main, p_main, g_gate, fb_pad, tri, sel, eq, ek, cq, ck)


def _flash_sweep(qi, n_streams, score_fn, value_fn, visible, sa_ref, sb_ref, m_ref, acc_ref):
    m_ref[...] = jnp.full(m_ref.shape, NEG_INIT, F32)
    acc_ref[...] = jnp.zeros(acc_ref.shape, F32)

    def consume(src_ref, st, j, masked):
        s = src_ref[st]
        if masked:
            s = jnp.where(visible, s, -jnp.inf)
        m = m_ref[st]
        m_new = jnp.maximum(m, jnp.max(s, axis=0, keepdims=True))
        p = jnp.exp2(s - m_new).astype(BF16)
        acc_ref[st] = jnp.exp2(m - m_new) * acc_ref[st] + _dot(value_fn(st, j), p)
        m_ref[st] = m_new

    def stage(dst_ref, src_ref, j_next, j_cur):
        for st in range(n_streams):
            dst_ref[st] = score_fn(st, j_next)
            consume(src_ref, st, j_cur, False)

    for st in range(n_streams):
        sa_ref[st] = score_fn(st, 0)

    def pair(p, carry):
        stage(sb_ref, sa_ref, 2 * p + 1, 2 * p)
        stage(sa_ref, sb_ref, 2 * p + 2, 2 * p + 1)
        return carry

    lax.fori_loop(0, qi // 2, pair, 0)

    @pl.when(qi % 2 == 0)
    def _():
        for st in range(n_streams):
            consume(sa_ref, st, qi, True)

    @pl.when(qi % 2 == 1)
    def _():
        stage(sb_ref, sa_ref, qi, qi - 1)
        for st in range(n_streams):
            consume(sb_ref, st, qi, True)


def _flash_scratch(n_streams, tile):
    return [pltpu.VMEM((n_streams, tile, tile), F32), pltpu.VMEM((n_streams, tile, tile), F32),
            pltpu.VMEM((n_streams, 1, tile), F32), pltpu.VMEM((n_streams, V_ROWS, tile), F32)]


def _finish(acc, dim):
    return acc[:dim] / acc[dim:dim + 1]


def _transposed(o, tile):
    dim = o.shape[0]
    padded = jnp.concatenate([o, jnp.zeros((LANES - dim, tile), F32)], axis=0)
    return padded.T[:, :dim]


def _key_block(j, tile):
    return pl.ds(pl.multiple_of(j * tile, tile), tile)


def _fox_attn_kernel(q_ref, k_ref, vt_ref, o_ref, sa_ref, sb_ref, m_ref, acc_ref, *, tile):
    qi = pl.program_id(1)
    row = lax.broadcasted_iota(jnp.int32, (tile, tile), 0)
    col = lax.broadcasted_iota(jnp.int32, (tile, tile), 1)
    visible = row <= col
    _flash_sweep(qi, FOX_HEADS,
                 lambda h, j: _dot_nt(k_ref[0, h, _key_block(j, tile), :], q_ref[0, h]),
                 lambda h, j: vt_ref[0, h, :, _key_block(j, tile)],
                 visible, sa_ref, sb_ref, m_ref, acc_ref)
    o = jnp.concatenate([_finish(acc_ref[h], FOX_DIM) for h in range(FOX_HEADS)], axis=0)
    o_ref[...] = o.T.astype(o_ref.dtype)


def _fox_attn(q_aug, k_aug, vt_aug, tile):
    batch, heads, seq, _ = q_aug.shape
    return pl.pallas_call(
        functools.partial(_fox_attn_kernel, tile=tile), grid=(batch, seq // tile),
        in_specs=[pl.BlockSpec((1, heads, tile, LANES), lambda b, i: (b, 0, i, 0)),
                  pl.BlockSpec((1, heads, seq, LANES), lambda b, i: (b, 0, 0, 0)),
                  pl.BlockSpec((1, heads, V_ROWS, seq), lambda b, i: (b, 0, 0, 0))],
        out_specs=pl.BlockSpec((tile, heads * FOX_DIM), lambda b, i: (b * (seq // tile) + i, 0)),
        out_shape=jax.ShapeDtypeStruct((batch * seq, heads * FOX_DIM), BF16),
        scratch_shapes=_flash_scratch(heads, tile),
        compiler_params=_params(("parallel", "arbitrary")), name="fox_attn")(q_aug, k_aug, vt_aug)


def _diff_attn_kernel(q_ref, k_ref, vt_ref, lam_ref, g_ref, o_ref, q12_ref, sa_ref, sb_ref, m_ref, acc_ref,
                      *, tile, lam_init):
    qi = pl.program_id(1)
    lane = lax.broadcasted_iota(jnp.int32, q_ref.shape, 1)
    q = q_ref[...]
    for st in range(2 * DIFF_HEADS):
        keep = (lane >= DIFF_QK_DIM * st) & (lane < DIFF_QK_DIM * (st + 1))
        q12_ref[st] = jnp.where(keep, q, jnp.zeros_like(q))
    row = lax.broadcasted_iota(jnp.int32, (tile, tile), 0)
    col = lax.broadcasted_iota(jnp.int32, (tile, tile), 1)
    visible = (row // HGRN_CHUNK) <= (col // HGRN_CHUNK)
    _flash_sweep(qi, 2 * DIFF_HEADS,
                 lambda st, j: _dot_nt(k_ref[_key_block(j, tile), :], q12_ref[st]),
                 lambda st, j: vt_ref[0, st // 2, :, _key_block(j, tile)],
                 visible, sa_ref, sb_ref, m_ref, acc_ref)

    lp = lam_ref[...]
    lam = (jnp.exp(jnp.sum(lp[0:1] * lp[1:2], axis=1, keepdims=True))
           - jnp.exp(jnp.sum(lp[2:3] * lp[3:4], axis=1, keepdims=True)) + lam_init)
    outs = []
    for h in range(DIFF_HEADS):
        o = _finish(acc_ref[2 * h], DIFF_V_DIM) - lam * _finish(acc_ref[2 * h + 1], DIFF_V_DIM)
        outs.append(o * lax.rsqrt(jnp.mean(o * o, axis=0, keepdims=True) + LN_EPS))
    y = jnp.concatenate(outs, axis=0).T * g_ref[...] * (1.0 - lam_init)
    o_ref[...] = y.astype(o_ref.dtype)


def _diff_attn(p_main, dvt_aug, lam_pack, norm_g, lam_init, batch, seq, tile):
    heads = DIFF_HEADS
    width = heads * DIFF_V_DIM
    nq = seq // tile
    return pl.pallas_call(
        functools.partial(_diff_attn_kernel, tile=tile, lam_init=lam_init), grid=(batch, nq),
        in_specs=[pl.BlockSpec((tile, width), lambda b, i: (b * nq + i, C_DQ // width)),
                  pl.BlockSpec((seq, width), lambda b, i: (b, C_DK // width)),
                  pl.BlockSpec((1, heads, V_ROWS, seq), lambda b, i: (b, 0, 0, 0)),
                  pl.BlockSpec((8, LANES), lambda b, i: (0, 0)),
                  pl.BlockSpec((1, width), lambda b, i: (0, 0))],
        out_specs=pl.BlockSpec((tile, width), lambda b, i: (b * nq + i, 0)),
        out_shape=jax.ShapeDtypeStruct((batch * seq, width), BF16),
        scratch_shapes=[pltpu.VMEM((2 * heads, tile, width), BF16)] + _flash_scratch(2 * heads, tile),
        compiler_params=_params(("parallel", "arbitrary")), name="diff_attn",
    )(p_main, p_main, dvt_aug, lam_pack, norm_g)


def _hgrn_kernel(hq_ref, hi_ref, hg_ref, hf_ref, lb_ref, g_ref, tri_ref, o_ref, state_ref, *, layer, tc):
    @pl.when(pl.program_id(1) == 0)
    def _():
        state_ref[...] = jnp.zeros_like(state_ref)

    lb_all = lb_ref[...]
    e = jnp.exp(lb_all - jnp.max(lb_all, axis=0, keepdims=True))
    prob = e / jnp.sum(e, axis=0, keepdims=True)
    lb_row = jnp.zeros((1, HGRN_W), F32)
    for j in range(1, layer + 1):
        lb_row = lb_row + prob[j:j + 1]
    lb_row = jnp.maximum(lb_row, 0.0)

    c = HGRN_CHUNK
    n_chunks = tc // c
    half = c // 2 - 1
    r_i = lax.broadcasted_iota(jnp.int32, (c, c), 0)
    c_i = lax.broadcasted_iota(jnp.int32, (c, c), 1)
    causal = r_i >= c_i

    log_lb = jnp.log(lb_row)
    log_1m = jnp.log(1.0 - lb_row)
    z = hf_ref[...]
    ez = jnp.exp(-jnp.abs(z))
    b_term = log_1m + jnp.minimum(z, 0.0) - jnp.log(1.0 + ez)
    logf = jnp.maximum(log_lb, b_term) + jnp.log(1.0 + jnp.exp(-jnp.abs(log_lb - b_term)))
    kk = (1.0 - lb_row) * jnp.where(z >= 0.0, ez, 1.0) / (1.0 + ez)
    xq = hq_ref[...].astype(F32)
    q = xq * (0.5 + 0.5 * jnp.tanh(0.5 * xq))
    xg = hg_ref[...].astype(F32)
    gate = xg * (0.5 + 0.5 * jnp.tanh(0.5 * xg))
    v = hi_ref[...]
    vt = v.astype(F32).T.astype(BF16)

    tri = tri_ref[...]
    g_hi, g_mid, g_lo = _split3(logf)
    bc = _dot(tri, g_hi) + _dot(tri, g_mid) + _dot(tri, g_lo)

    rows = lambda r: jnp.concatenate(
        [jnp.broadcast_to(bc[ci * c + r:ci * c + r + 1], (c, HGRN_W)) for ci in range(n_chunks)], axis=0)
    ref_full = rows(half)
    last_full = rows(c - 1)
    q_mid = q * jnp.exp(bc - ref_full)
    k_mid = kk * jnp.exp(ref_full - bc)
    q_in = (q_mid * jnp.exp(ref_full)).astype(BF16)
    k_out = (k_mid * jnp.exp(last_full - ref_full)).astype(BF16)
    q_mid = q_mid.astype(BF16)
    k_mid = k_mid.astype(BF16)

    units = [(ci, h) for ci in range(n_chunks) for h in range(HGRN_HEADS)]
    sl = lambda ci, h: (slice(ci * c, (ci + 1) * c), slice(h * HGRN_DIM, (h + 1) * HGRN_DIM))
    scores = {u: jnp.where(causal, _dot_nt(q_mid[sl(*u)], k_mid[sl(*u)]), 0.0).astype(BF16) for u in units}
    intra = {u: _dot(scores[u], v[sl(*u)]) for u in units}
    updates = {(ci, h): _dot(vt[h * HGRN_DIM:(h + 1) * HGRN_DIM, ci * c:(ci + 1) * c], k_out[sl(ci, h)])
               for ci, h in units}

    states = [state_ref[h] for h in range(HGRN_HEADS)]
    for ci in range(n_chunks):
        for h in range(HGRN_HEADS):
            rs, hs = sl(ci, h)
            o = intra[(ci, h)] + _dot_nt(q_in[rs, hs], states[h].astype(BF16))
            decay = jnp.exp(bc[ci * c + c - 1:ci * c + c, hs])
            states[h] = states[h] * decay + updates[(ci, h)]
            y = o * lax.rsqrt(jnp.mean(o * o, axis=1, keepdims=True) + LN_EPS) * g_ref[...] * gate[rs, hs]
            o_ref[rs, hs] = y.astype(o_ref.dtype)
    for h in range(HGRN_HEADS):
        state_ref[h] = states[h]


def _hgrn(p_main, g_gate, hgrn_lb, norm_g, layer, batch, seq, tc):
    ns = seq // tc
    chunk_tri = np.tril(np.ones((HGRN_CHUNK, HGRN_CHUNK), np.float32))
    tri = jnp.asarray(np.kron(np.eye(tc // HGRN_CHUNK, dtype=np.float32), chunk_tri), BF16)
    col = lambda c0: pl.BlockSpec((tc, HGRN_W), lambda b, s: (b * ns + s, c0 // HGRN_W))
    return pl.pallas_call(
        functools.partial(_hgrn_kernel, layer=layer, tc=tc), grid=(batch, ns),
        in_specs=[col(C_HQ), col(C_HI), col(C_HG),
                  pl.BlockSpec((tc, HGRN_W), lambda b, s: (b * ns + s, 0)),
                  pl.BlockSpec((DEPTH, HGRN_W), lambda b, s: (0, 0)),
                  pl.BlockSpec((1, HGRN_DIM), lambda b, s: (0, 0)),
                  pl.BlockSpec((tc, tc), lambda b, s: (0, 0))],
        out_specs=pl.BlockSpec((tc, HGRN_W), lambda b, s: (b * ns + s, 0)),
        out_shape=jax.ShapeDtypeStruct((batch * seq, HGRN_W), BF16),
        scratch_shapes=[pltpu.VMEM((HGRN_HEADS, HGRN_DIM, HGRN_DIM), F32)],
        compiler_params=_params(("parallel", "arbitrary")), name="hgrn2",
    )(p_main, p_main, p_main, g_gate, hgrn_lb, norm_g, tri)


def _proj_ln_kernel(yf_ref, yd_ref, yh_ref, w_ref, h_ref, g_ref, b_ref, ho_ref, hbo_ref):
    fw = yf_ref.shape[1]
    dw = fw + yd_ref.shape[1]
    mix = _dot(yf_ref[...], w_ref[0:fw]) + _dot(yd_ref[...], w_ref[fw:dw]) + _dot(yh_ref[...], w_ref[dw:])
    out = _layer_norm(ALPHA * h_ref[...] + mix, g_ref[...], b_ref[...])
    ho_ref[...] = out
    hbo_ref[...] = out.astype(BF16)


def _proj_ln(y_fox, y_diff, y_hgrn, w, h, g, b, tm):
    n, d = h.shape
    row = lambda width: pl.BlockSpec((tm, width), lambda i: (i, 0))
    vec = pl.BlockSpec((1, d), lambda i: (0, 0))
    return pl.pallas_call(
        _proj_ln_kernel, grid=(n // tm,),
        in_specs=[row(y_fox.shape[1]), row(y_diff.shape[1]), row(y_hgrn.shape[1]),
                  pl.BlockSpec(w.shape, lambda i: (0, 0)), row(d), vec, vec],
        out_specs=[row(d), row(d)],
        out_shape=[jax.ShapeDtypeStruct((n, d), F32), jax.ShapeDtypeStruct((n, d), BF16)],
        compiler_params=_params(("parallel",)), name="out_proj_ln")(y_fox, y_diff, y_hgrn, w, h, g, b)


def _mem_attn_kernel(hb_ref, h_ref, wq_ref, k_ref, v_ref, wo_ref, g_ref, b_ref, ho_ref, hbo_ref):
    q = _dot(hb_ref[...], wq_ref[...])
    outs = []
    for hd in range(MEM_HEADS):
        hs = slice(hd * MEM_DIM, (hd + 1) * MEM_DIM)
        s = _dot_nt(q[:, hs].astype(BF16), k_ref[:, hs])
        p = jnp.exp2(s - jnp.max(s, axis=1, keepdims=True))
        o = _dot(p.astype(BF16), v_ref[:, hs]) / jnp.sum(p, axis=1, keepdims=True)
        outs.append(o.astype(BF16))
    att = jnp.concatenate(outs, axis=1)
    z = ALPHA * h_ref[...] + _dot(att, wo_ref[...])
    out = _layer_norm(z, g_ref[...], b_ref[...])
    ho_ref[...] = out
    hbo_ref[...] = out.astype(BF16)


def _mem_attn(hb, h, wq, kmem, vmem, wo, g, b, batch, seq, tm):
    n, d = h.shape
    ns = seq // tm
    mem_len = kmem.shape[0] // batch
    row = pl.BlockSpec((tm, d), lambda bi, s: (bi * ns + s, 0))
    full = pl.BlockSpec((d, d), lambda bi, s: (0, 0))
    mem = pl.BlockSpec((mem_len, d), lambda bi, s: (bi, 0))
    vec = pl.BlockSpec((1, d), lambda bi, s: (0, 0))
    return pl.pallas_call(
        _mem_attn_kernel, grid=(batch, ns),
        in_specs=[row, row, full, mem, mem, full, vec, vec], out_specs=[row, row],
        out_shape=[jax.ShapeDtypeStruct((n, d), F32), jax.ShapeDtypeStruct((n, d), BF16)],
        compiler_params=_params(("parallel", "arbitrary")), name="mem_attn")(hb, h, wq, kmem, vmem, wo, g, b)


def _top2_sum(a, b, c, d):
    hi1, lo1 = jnp.maximum(a, b), jnp.minimum(a, b)
    hi2, lo2 = jnp.maximum(c, d), jnp.minimum(c, d)
    return jnp.maximum(hi1, hi2) + jnp.maximum(jnp.minimum(hi1, hi2), jnp.maximum(lo1, lo2))


def _router_kernel(h_ref, rw_ref, rb_ref, gd_ref, rows_ref):
    xh, xm, xl = _split3(h_ref[...])
    wh, wm, wl = rw_ref[0], rw_ref[1], rw_ref[2]
    logits = (_dot_nt(wh, xh) + _dot_nt(wh, xm) + _dot_nt(wm, xh)
              + _dot_nt(wh, xl) + _dot_nt(wm, xm) + _dot_nt(wl, xh))
    scores = _sigmoid(logits)
    biased = scores + rb_ref[...]
    s_rows = [scores[e:e + 1] for e in range(N_EXPERTS)]
    b_rows = [biased[e:e + 1] for e in range(N_EXPERTS)]
    group_score = [_top2_sum(*b_rows[4 * g:4 * g + 4]) for g in range(N_GROUPS)]
    best = group_score[0]
    best_idx = jnp.zeros_like(best, dtype=jnp.int32)
    for g in range(1, N_GROUPS):
        better = group_score[g] > best
        best_idx = jnp.where(better, g, best_idx)
        best = jnp.where(better, group_score[g], best)
    picked = []
    for e in range(N_EXPERTS):
        g = e // EXPERTS_PER_GROUP
        rank = jnp.zeros_like(best)
        for j in range(4 * g, 4 * g + 4):
            if j == e:
                continue
            ahead = (b_rows[j] > b_rows[e]) | (b_rows[j] == b_rows[e]) if j < e else (b_rows[j] > b_rows[e])
            rank = rank + jnp.where(ahead, 1.0, 0.0)
        picked.append((best_idx == g) & (rank < 2.0))
    denom = jnp.zeros_like(best)
    for e in range(N_EXPERTS):
        denom = denom + jnp.where(picked[e], s_rows[e], 0.0)
    rows_ref[...] = jnp.zeros_like(rows_ref)
    for e in range(N_EXPERTS):
        rows_ref[e:e + 1, :] = jnp.where(picked[e], s_rows[e] / denom, 0.0)
    gd_ref[...] = rows_ref[...].T


def _router(h, rw_parts, rb_col, tm):
    n, d = h.shape
    return pl.pallas_call(
        _router_kernel, grid=(n // tm,),
        in_specs=[pl.BlockSpec((tm, d), lambda i: (i, 0)),
                  pl.BlockSpec((3, N_EXPERTS, d), lambda i: (0, 0, 0)),
                  pl.BlockSpec((N_EXPERTS, 1), lambda i: (0, 0))],
        out_specs=pl.BlockSpec((tm, LANES), lambda i: (i, 0)),
        out_shape=jax.ShapeDtypeStruct((n, LANES), F32),
        scratch_shapes=[pltpu.VMEM((LANES, tm), F32)],
        compiler_params=_params(("parallel",)), name="router")(h, rw_parts, rb_col)


def _moe_kernel(hb_ref, h_ref, gd_ref, w1_ref, w3_ref, w2_ref, g_ref, b_ref, ho_ref, hbo_ref, acc_ref):
    e = pl.program_id(1)

    @pl.when(e == 0)
    def _():
        acc_ref[...] = jnp.zeros_like(acc_ref)

    x = hb_ref[...]
    a = _dot(x, w1_ref[0])
    hid = (a * _sigmoid(a)) * _dot(x, w3_ref[0])
    y = _dot(hid.astype(BF16), w2_ref[0])
    gd = gd_ref[...]
    lane = lax.broadcasted_iota(jnp.int32, gd.shape, 1)
    gate = jnp.sum(jnp.where(lane == e, gd, 0.0), axis=1, keepdims=True)
    acc_ref[...] += gate * y

    @pl.when(e == N_EXPERTS - 1)
    def _():
        out = _layer_norm(ALPHA * h_ref[...] + acc_ref[...], g_ref[...], b_ref[...])
        ho_ref[...] = out
        hbo_ref[...] = out.astype(BF16)


def _moe(hb, h, gd, w1, w3, w2, g, b, tm):
    n, d = h.shape
    ff = w1.shape[2]
    row = lambda width: pl.BlockSpec((tm, width), lambda i, e: (i, 0))
    vec = pl.BlockSpec((1, d), lambda i, e: (0, 0))
    return pl.pallas_call(
        _moe_kernel, grid=(n // tm, N_EXPERTS),
        in_specs=[row(d), row(d), row(LANES),
                  pl.BlockSpec((1, d, ff), lambda i, e: (e, 0, 0)),
                  pl.BlockSpec((1, d, ff), lambda i, e: (e, 0, 0)),
                  pl.BlockSpec((1, ff, d), lambda i, e: (e, 0, 0)), vec, vec],
        out_specs=[row(d), row(d)],
        out_shape=[jax.ShapeDtypeStruct((n, d), F32), jax.ShapeDtypeStruct((n, d), BF16)],
        scratch_shapes=[pltpu.VMEM((tm, d), F32)],
        compiler_params=_params(("parallel", "arbitrary")), name="moe")(hb, h, gd, w1, w3, w2, g, b)


def _pack_in_weights(w_in):
    sizes = (256, 256, 256, FOX_HEADS, 256, 256, 256, HGRN_W, HGRN_W, HGRN_W, HGRN_W)
    offs = np.cumsum((0,) + sizes)
    fq, fk, fv, ff, dq, dk, dv, hq, hf, hi, hg = (w_in[:, offs[i]:offs[i + 1]] for i in range(len(sizes)))
    fox_scale = FOX_DIM ** -0.5 * LOG2E
    diff_scale = DIFF_QK_DIM ** -0.5 * LOG2E
    main = jnp.concatenate([fq * fox_scale, fk, fv, dq * diff_scale, dk, dv, hq, hi, hg], axis=1).astype(BF16)
    ff_pad = jnp.zeros((w_in.shape[0], LANES), w_in.dtype).at[:, :FOX_HEADS].set(ff)
    gate = jnp.concatenate([hf, ff_pad], axis=1).astype(BF16)
    return main, gate


def kernel(x, mem, ln_in_g, ln_in_b, w_in, fox_fb, lam_q1, lam_k1, lam_q2, lam_k2, diff_norm_g, hgrn_lb,
           hgrn_norm_g, w_out, mem_wq, mem_wk, mem_wv, mem_wo, router_w, router_b, w1, w3, w2, ln_g, ln_b):
    batch, seq, d = x.shape
    n = batch * seq
    tm = min(512, seq)
    attn_tile = min(512, seq)
    vec = lambda v: v.reshape(1, -1).astype(F32)

    h, hb = _entry_ln(x.reshape(n, d), vec(ln_in_g), vec(ln_in_b), tm)
    memb = mem.reshape(-1, d).astype(BF16)
    rw_parts = jnp.stack(_split3(router_w.T.astype(F32)))
    rb_col = router_b.reshape(N_EXPERTS, 1).astype(F32)
    hgrn_lb = hgrn_lb.astype(F32)

    for i in range(DEPTH):
        lam_init = 0.8 - 0.6 * math.exp(-0.3 * i)
        w_main, w_gate = _pack_in_weights(w_in[i])
        p_main, g_gate = _in_proj(hb, w_main, w_gate, tm)

        q_aug, k_aug, fvt, dvt = _fox_prep(p_main, g_gate, fox_fb[i].astype(F32), batch, seq, min(512, seq))
        y_fox = _fox_attn(q_aug, k_aug, fvt, attn_tile)

        lam_pack = jnp.zeros((8, LANES), F32)
        for r, v in enumerate((lam_q1[i], lam_k1[i], lam_q2[i], lam_k2[i])):
            lam_pack = lam_pack.at[r, :DIFF_QK_DIM].set(v.astype(F32))
        diff_g = jnp.tile(vec(diff_norm_g[i]), (1, DIFF_HEADS))
        y_diff = _diff_attn(p_main, dvt, lam_pack, diff_g, lam_init, batch, seq, attn_tile)

        y_hgrn = _hgrn(p_main, g_gate, hgrn_lb, vec(hgrn_norm_g[i]), i, batch, seq, min(256, seq))
        h, hb = _proj_ln(y_fox, y_diff, y_hgrn, w_out[i].astype(BF16), h, vec(ln_g[i, 0]), vec(ln_b[i, 0]), tm)

        kmem = _matmul(memb, mem_wk[i].astype(BF16), BF16, memb.shape[0], 512, "mem_k")
        vmem = _matmul(memb, mem_wv[i].astype(BF16), BF16, memb.shape[0], 512, "mem_v")
        wq = (mem_wq[i] * (MEM_DIM ** -0.5 * LOG2E)).astype(BF16)
        h, hb = _mem_attn(hb, h, wq, kmem, vmem, mem_wo[i].astype(BF16), vec(ln_g[i, 1]), vec(ln_b[i, 1]),
                          batch, seq, tm)

        gd = _router(h, rw_parts, rb_col, tm)
        h, hb = _moe(hb, h, gd, w1[i].astype(BF16), w3[i].astype(BF16), w2[i].astype(BF16),
                     vec(ln_g[i, 2]), vec(ln_b[i, 2]), min(1024, n))
    return h.reshape(batch, seq, d)
```

```python
import functools
import math

import numpy as np
import jax
import jax.numpy as jnp
from jax import lax
from jax.experimental import pallas as pl
from jax.experimental.pallas import tpu as pltpu

F32 = jnp.float32
BF16 = jnp.bfloat16

D_MODEL = 1024
DEPTH = 2
FOX_DIM = 64
FOX_HEADS = 4
DIFF_QK_DIM = 32
DIFF_V_DIM = 64
DIFF_HEADS = 4
HGRN_DIM = 128
HGRN_HEADS = 4
HGRN_W = HGRN_DIM * HGRN_HEADS
HGRN_CHUNK = 64
MEM_HEADS = 4
MEM_DIM = D_MODEL // MEM_HEADS
N_EXPERTS = 16
N_GROUPS = 4
EXPERTS_PER_GROUP = 4
EXPERT_FF = D_MODEL // 2
ALPHA = (2 * DEPTH) ** 0.25
LN_EPS = 1e-5
LOG2E = 1.4426950408889634
NEG_INIT = -1e30
LANES = 128
V_ROWS = 80
VMEM_LIMIT = 56 * 1024 * 1024

C_FQ, C_FK, C_FV, C_DQ, C_DK, C_DV, C_HQ, C_HI, C_HG = (0, 256, 512, 768, 1024, 1280, 1536, 2048, 2560)
MAIN_W = 3072
GATE_W = HGRN_W + LANES


def _params(sem, vmem=VMEM_LIMIT, flags=None):
    return pltpu.CompilerParams(dimension_semantics=sem, vmem_limit_bytes=vmem, flags=flags)


def _split3(x):
    hi = x.astype(BF16)
    r = x - hi.astype(F32)
    mid = r.astype(BF16)
    lo = (r - mid.astype(F32)).astype(BF16)
    return hi, mid, lo


def _dot(a, b):
    return jnp.dot(a, b, preferred_element_type=F32)


def _dot_nt(a, b):
    return lax.dot_general(a, b, (((1,), (1,)), ((), ())), preferred_element_type=F32)


def _sigmoid(x):
    return 1.0 / (1.0 + jnp.exp(-x))


def _log_sigmoid(x):
    return jnp.minimum(x, 0.0) - jnp.log(1.0 + jnp.exp(-jnp.abs(x)))


def _layer_norm(x, g, b):
    mu = jnp.mean(x, axis=-1, keepdims=True)
    xc = x - mu
    var = jnp.mean(xc * xc, axis=-1, keepdims=True)
    return xc * lax.rsqrt(var + LN_EPS) * g + b


def _ln_kernel(x_ref, g_ref, b_ref, h_ref, hb_ref):
    y = _layer_norm(x_ref[...], g_ref[...], b_ref[...])
    h_ref[...] = y
    hb_ref[...] = y.astype(BF16)


def _entry_ln(x, g, b, tm):
    n, d = x.shape
    row = pl.BlockSpec((tm, d), lambda i: (i, 0))
    vec = pl.BlockSpec((1, d), lambda i: (0, 0))
    return pl.pallas_call(
        _ln_kernel, grid=(n // tm,), in_specs=[row, vec, vec], out_specs=[row, row],
        out_shape=[jax.ShapeDtypeStruct((n, d), F32), jax.ShapeDtypeStruct((n, d), BF16)],
        compiler_params=_params(("parallel",)), name="entry_ln")(x, g, b)


def _mm_kernel(x_ref, w_ref, o_ref):
    o_ref[...] = _dot(x_ref[...], w_ref[...]).astype(o_ref.dtype)


def _matmul(x, w, out_dtype, tm, tn, name):
    n, k = x.shape
    m = w.shape[1]
    return pl.pallas_call(
        _mm_kernel, grid=(n // tm, m // tn),
        in_specs=[pl.BlockSpec((tm, k), lambda i, j: (i, 0)), pl.BlockSpec((k, tn), lambda i, j: (0, j))],
        out_specs=pl.BlockSpec((tm, tn), lambda i, j: (i, j)),
        out_shape=jax.ShapeDtypeStruct((n, m), out_dtype),
        compiler_params=_params(("parallel", "arbitrary")), name=name)(x, w)


IN_PROJ_CHUNK = 512


def _in_proj_kernel(x_ref, wm_ref, wg_ref, pm_ref, pg_ref):
    x = x_ref[...]
    for c0 in range(0, MAIN_W, IN_PROJ_CHUNK):
        cs = slice(c0, c0 + IN_PROJ_CHUNK)
        pm_ref[:, cs] = _dot(x, wm_ref[:, cs]).astype(BF16)
    pg_ref[...] = _dot(x, wg_ref[...])


def _in_proj(hb, w_main, w_gate, tm):
    n, d = hb.shape
    return pl.pallas_call(
        _in_proj_kernel, grid=(n // tm,),
        in_specs=[pl.BlockSpec((tm, d), lambda i: (i, 0)),
                  pl.BlockSpec((d, MAIN_W), lambda i: (0, 0)), pl.BlockSpec((d, GATE_W), lambda i: (0, 0))],
        out_specs=[pl.BlockSpec((tm, MAIN_W), lambda i: (i, 0)), pl.BlockSpec((tm, GATE_W), lambda i: (i, 0))],
        out_shape=[jax.ShapeDtypeStruct((n, MAIN_W), BF16), jax.ShapeDtypeStruct((n, GATE_W), F32)],
        compiler_params=_params(("parallel",)), name="in_proj")(hb, w_main, w_gate)


def _value_rows(v_ref, vt_ref, heads, dim):
    ts = v_ref.shape[0]
    vt = v_ref[...].astype(F32).T.astype(BF16)
    tail_row = lax.broadcasted_iota(jnp.int32, (V_ROWS - dim, ts), 0)
    tail = jnp.where(tail_row == 0, 1.0, 0.0).astype(BF16)
    for h in range(heads):
        vt_ref[0, h, 0:dim, :] = vt[h * dim:(h + 1) * dim]
        vt_ref[0, h, dim:V_ROWS, :] = tail


def _fox_prep_kernel(fq_ref, fk_ref, fv_ref, dv_ref, ff_ref, fb_ref, tri_ref, sel_ref, eq_ref, ek_ref, cq_ref,
                     ck_ref, qa_ref, ka_ref, fvt_ref, dvt_ref, carry_ref):
    @pl.when(pl.program_id(1) == 0)
    def _():
        carry_ref[...] = jnp.zeros_like(carry_ref)

    _value_rows(fv_ref, fvt_ref, FOX_HEADS, FOX_DIM)
    _value_rows(dv_ref, dvt_ref, DIFF_HEADS, DIFF_V_DIM)

    logf = _log_sigmoid(ff_ref[...] + fb_ref[...])
    tri = tri_ref[...]
    hi, mid, lo = _split3(logf)
    cum = _dot(tri, hi) + _dot(tri, mid) + _dot(tri, lo) + carry_ref[...]
    ts = cum.shape[0]
    carry_ref[...] = cum[ts - 1:ts, :]
    parts = jnp.concatenate(_split3(cum * LOG2E), axis=1)
    fq = fq_ref[...]
    fk = fk_ref[...]
    for h in range(FOX_HEADS):
        pair = slice((h // 2) * LANES, (h // 2 + 1) * LANES)
        sel = sel_ref[h % 2]
        qa = _dot(fq[:, pair], sel) + _dot(parts, eq_ref[h]) + cq_ref[...]
        ka = _dot(fk[:, pair], sel) + _dot(parts, ek_ref[h]) + ck_ref[...]
        qa_ref[0, h] = qa.astype(BF16)
        ka_ref[0, h] = ka.astype(BF16)


def _fox_prep_consts():
    sel = np.zeros((2, LANES, LANES), np.float32)
    for half in range(2):
        for d in range(FOX_DIM):
            sel[half, half * FOX_DIM + d, d] = 1.0
    eq = np.zeros((FOX_HEADS, 3 * LANES, LANES), np.float32)
    ek = np.zeros((FOX_HEADS, 3 * LANES, LANES), np.float32)
    for h in range(FOX_HEADS):
        for p in range(3):
            eq[h, p * LANES + h, FOX_DIM + p] = 1.0
            ek[h, p * LANES + h, FOX_DIM + 3 + p] = -1.0
    cq = np.zeros((1, LANES), np.float32)
    ck = np.zeros((1, LANES), np.float32)
    cq[0, FOX_DIM + 3:FOX_DIM + 6] = 1.0
    ck[0, FOX_DIM:FOX_DIM + 3] = 1.0
    return (jnp.asarray(sel, BF16), jnp.asarray(eq, BF16), jnp.asarray(ek, BF16), jnp.asarray(cq), jnp.asarray(ck))


def _fox_prep(p_main, g_gate, fb, batch, seq, ts):
    ns = seq // ts
    tri = jnp.asarray(np.tril(np.ones((ts, ts), np.float32)), BF16)
    sel, eq, ek, cq, ck = _fox_prep_consts()
    fb_pad = jnp.zeros((1, LANES), F32).at[0, :FOX_HEADS].set(fb)
    const2 = lambda shape: pl.BlockSpec(shape, lambda b, s: (0,) * len(shape))
    out_spec = pl.BlockSpec((1, FOX_HEADS, ts, LANES), lambda b, s: (b, 0, s, 0))
    out_sds = jax.ShapeDtypeStruct((batch, FOX_HEADS, seq, LANES), BF16)
    vt_spec = pl.BlockSpec((1, FOX_HEADS, V_ROWS, ts), lambda b, s: (b, 0, 0, s))
    vt_sds = jax.ShapeDtypeStruct((batch, FOX_HEADS, V_ROWS, seq), BF16)
    col = lambda c0: pl.BlockSpec((ts, 256), lambda b, s: (b * ns + s, c0 // 256))
    return pl.pallas_call(
        _fox_prep_kernel, grid=(batch, ns),
        in_specs=[col(C_FQ), col(C_FK), col(C_FV), col(C_DV),
                  pl.BlockSpec((ts, LANES), lambda b, s: (b * ns + s, HGRN_W // LANES)),
                  const2((1, LANES)), const2((ts, ts)), const2((2, LANES, LANES)),
                  const2((FOX_HEADS, 3 * LANES, LANES)), const2((FOX_HEADS, 3 * LANES, LANES)),
                  const2((1, LANES)), const2((1, LANES))],
        out_specs=[out_spec, out_spec, vt_spec, vt_spec], out_shape=[out_sds, out_sds, vt_sds, vt_sds],
        scratch_shapes=[pltpu.VMEM((1, LANES), F32)],
        compiler_params=_params(("arbitrary", "arbitrary")), name="fox_prep",
    )(p_main, p_main, p_main, p_main, g_gate, fb_pad, tri, sel, eq, ek, cq, ck)


def _flash_sweep(qi, n_streams, score_fn, value_fn, visible, sa_ref, sb_ref, m_ref, acc_ref):
    m_ref[...] = jnp.full(m_ref.shape, NEG_INIT, F32)
    acc_ref[...] = jnp.zeros(acc_ref.shape, F32)

    def consume(src_ref, st, j, masked):
        s = src_ref[st]
        if masked:
            s = jnp.where(visible, s, -jnp.inf)
        m = m_ref[st]
        m_new = jnp.maximum(m, jnp.max(s, axis=0, keepdims=True))
        p = jnp.exp2(s - m_new).astype(BF16)
        acc_ref[st] = jnp.exp2(m - m_new) * acc_ref[st] + _dot(value_fn(st, j), p)
        m_ref[st] = m_new

    def stage(dst_ref, src_ref, j_next, j_cur):
        for st in range(n_streams):
            dst_ref[st] = score_fn(st, j_next)
            consume(src_ref, st, j_cur, False)

    for st in range(n_streams):
        sa_ref[st] = score_fn(st, 0)

    def pair(p, carry):
        stage(sb_ref, sa_ref, 2 * p + 1, 2 * p)
        stage(sa_ref, sb_ref, 2 * p + 2, 2 * p + 1)
        return carry

    lax.fori_loop(0, qi // 2, pair, 0)

    @pl.when(qi % 2 == 0)
    def _():
        for st in range(n_streams):
            consume(sa_ref, st, qi, True)

    @pl.when(qi % 2 == 1)
    def _():
        stage(sb_ref, sa_ref, qi, qi - 1)
        for st in range(n_streams):
            consume(sb_ref, st, qi, True)


def _flash_scratch(n_streams, tile):
    return [pltpu.VMEM((n_streams, tile, tile), F32), pltpu.VMEM((n_streams, tile, tile), F32),
            pltpu.VMEM((n_streams, 1, tile), F32), pltpu.VMEM((n_streams, V_ROWS, tile), F32)]


def _finish(acc, dim):
    return acc[:dim] / acc[dim:dim + 1]


def _transposed(o, tile):
    dim = o.shape[0]
    padded = jnp.concatenate([o, jnp.zeros((LANES - dim, tile), F32)], axis=0)
    return padded.T[:, :dim]


def _key_block(j, tile):
    return pl.ds(pl.multiple_of(j * tile, tile), tile)


def _fox_attn_kernel(q_ref, k_ref, vt_ref, o_ref, sa_ref, sb_ref, m_ref, acc_ref, *, tile):
    qi = pl.program_id(1)
    row = lax.broadcasted_iota(jnp.int32, (tile, tile), 0)
    col = lax.broadcasted_iota(jnp.int32, (tile, tile), 1)
    visible = row <= col
    _flash_sweep(qi, FOX_HEADS,
                 lambda h, j: _dot_nt(k_ref[0, h, _key_block(j, tile), :], q_ref[0, h]),
                 lambda h, j: vt_ref[0, h, :, _key_block(j, tile)],
                 visible, sa_ref, sb_ref, m_ref, acc_ref)
    o = jnp.concatenate([_finish(acc_ref[h], FOX_DIM) for h in range(FOX_HEADS)], axis=0)
    o_ref[...] = o.T.astype(o_ref.dtype)


def _fox_attn(q_aug, k_aug, vt_aug, tile):
    batch, heads, seq, _ = q_aug.shape
    return pl.pallas_call(
        functools.partial(_fox_attn_kernel, tile=tile), grid=(batch, seq // tile),
        in_specs=[pl.BlockSpec((1, heads, tile, LANES), lambda b, i: (b, 0, i, 0)),
                  pl.BlockSpec((1, heads, seq, LANES), lambda b, i: (b, 0, 0, 0)),
                  pl.BlockSpec((1, heads, V_ROWS, seq), lambda b, i: (b, 0, 0, 0))],
        out_specs=pl.BlockSpec((tile, heads * FOX_DIM), lambda b, i: (b * (seq // tile) + i, 0)),
        out_shape=jax.ShapeDtypeStruct((batch * seq, heads * FOX_DIM), BF16),
        scratch_shapes=_flash_scratch(heads, tile),
        compiler_params=_params(("parallel", "arbitrary")), name="fox_attn")(q_aug, k_aug, vt_aug)


def _diff_attn_kernel(q_ref, k_ref, vt_ref, lam_ref, g_ref, o_ref, q12_ref, sa_ref, sb_ref, m_ref, acc_ref,
                      *, tile, lam_init):
    qi = pl.program_id(1)
    lane = lax.broadcasted_iota(jnp.int32, q_ref.shape, 1)
    q = q_ref[...]
    for st in range(2 * DIFF_HEADS):
        keep = (lane >= DIFF_QK_DIM * st) & (lane < DIFF_QK_DIM * (st + 1))
        q12_ref[st] = jnp.where(keep, q, jnp.zeros_like(q))
    row = lax.broadcasted_iota(jnp.int32, (tile, tile), 0)
    col = lax.broadcasted_iota(jnp.int32, (tile, tile), 1)
    visible = (row // HGRN_CHUNK) <= (col // HGRN_CHUNK)
    _flash_sweep(qi, 2 * DIFF_HEADS,
                 lambda st, j: _dot_nt(k_ref[_key_block(j, tile), :], q12_ref[st]),
                 lambda st, j: vt_ref[0, st // 2, :, _key_block(j, tile)],
                 visible, sa_ref, sb_ref, m_ref, acc_ref)

    lp = lam_ref[...]
    lam = (jnp.exp(jnp.sum(lp[0:1] * lp[1:2], axis=1, keepdims=True))
           - jnp.exp(jnp.sum(lp[2:3] * lp[3:4], axis=1, keepdims=True)) + lam_init)
    outs = []
    for h in range(DIFF_HEADS):
        o = _finish(acc_ref[2 * h], DIFF_V_DIM) - lam * _finish(acc_ref[2 * h + 1], DIFF_V_DIM)
        outs.append(o * lax.rsqrt(jnp.mean(o * o, axis=0, keepdims=True) + LN_EPS))
    y = jnp.concatenate(outs, axis=0).T * g_ref[...] * (1.0 - lam_init)
    o_ref[...] = y.astype(o_ref.dtype)


def _diff_attn(p_main, dvt_aug, lam_pack, norm_g, lam_init, batch, seq, tile):
    heads = DIFF_HEADS
    width = heads * DIFF_V_DIM
    nq = seq // tile
    return pl.pallas_call(
        functools.partial(_diff_attn_kernel, tile=tile, lam_init=lam_init), grid=(batch, nq),
        in_specs=[pl.BlockSpec((tile, width), lambda b, i: (b * nq + i, C_DQ // width)),
                  pl.BlockSpec((seq, width), lambda b, i: (b, C_DK // width)),
                  pl.BlockSpec((1, heads, V_ROWS, seq), lambda b, i: (b, 0, 0, 0)),
                  pl.BlockSpec((8, LANES), lambda b, i: (0, 0)),
                  pl.BlockSpec((1, width), lambda b, i: (0, 0))],
        out_specs=pl.BlockSpec((tile, width), lambda b, i: (b * nq + i, 0)),
        out_shape=jax.ShapeDtypeStruct((batch * seq, width), BF16),
        scratch_shapes=[pltpu.VMEM((2 * heads, tile, width), BF16)] + _flash_scratch(2 * heads, tile),
        compiler_params=_params(("parallel", "arbitrary")), name="diff_attn",
    )(p_main, p_main, dvt_aug, lam_pack, norm_g)


def _hgrn_kernel(hq_ref, hi_ref, hg_ref, hf_ref, lb_ref, g_ref, tri_ref, o_ref, state_ref, *, layer, tc):
    @pl.when(pl.program_id(1) == 0)
    def _():
        state_ref[...] = jnp.zeros_like(state_ref)

    lb_all = lb_ref[...]
    e = jnp.exp(lb_all - jnp.max(lb_all, axis=0, keepdims=True))
    prob = e / jnp.sum(e, axis=0, keepdims=True)
    lb_row = jnp.zeros((1, HGRN_W), F32)
    for j in range(1, layer + 1):
        lb_row = lb_row + prob[j:j + 1]
    lb_row = jnp.maximum(lb_row, 0.0)

    c = HGRN_CHUNK
    n_chunks = tc // c
    half = c // 2 - 1
    r_i = lax.broadcasted_iota(jnp.int32, (c, c), 0)
    c_i = lax.broadcasted_iota(jnp.int32, (c, c), 1)
    causal = r_i >= c_i

    log_lb = jnp.log(lb_row)
    log_1m = jnp.log(1.0 - lb_row)
    z = hf_ref[...]
    ez = jnp.exp(-jnp.abs(z))
    b_term = log_1m + jnp.minimum(z, 0.0) - jnp.log(1.0 + ez)
    logf = jnp.maximum(log_lb, b_term) + jnp.log(1.0 + jnp.exp(-jnp.abs(log_lb - b_term)))
    kk = (1.0 - lb_row) * jnp.where(z >= 0.0, ez, 1.0) / (1.0 + ez)
    xq = hq_ref[...].astype(F32)
    q = xq * (0.5 + 0.5 * jnp.tanh(0.5 * xq))
    xg = hg_ref[...].astype(F32)
    gate = xg * (0.5 + 0.5 * jnp.tanh(0.5 * xg))
    v = hi_ref[...]
    vt = v.astype(F32).T.astype(BF16)

    tri = tri_ref[...]
    g_hi, g_mid, g_lo = _split3(logf)
    bc = _dot(tri, g_hi) + _dot(tri, g_mid) + _dot(tri, g_lo)

    rows = lambda r: jnp.concatenate(
        [jnp.broadcast_to(bc[ci * c + r:ci * c + r + 1], (c, HGRN_W)) for ci in range(n_chunks)], axis=0)
    ref_full = rows(half)
    last_full = rows(c - 1)
    q_mid = q * jnp.exp(bc - ref_full)
    k_mid = kk * jnp.exp(ref_full - bc)
    q_in = (q_mid * jnp.exp(ref_full)).astype(BF16)
    k_out = (k_mid * jnp.exp(last_full - ref_full)).astype(BF16)
    q_mid = q_mid.astype(BF16)
    k_mid = k_mid.astype(BF16)

    units = [(ci, h) for ci in range(n_chunks) for h in range(HGRN_HEADS)]
    sl = lambda ci, h: (slice(ci * c, (ci + 1) * c), slice(h * HGRN_DIM, (h + 1) * HGRN_DIM))
    scores = {u: jnp.where(causal, _dot_nt(q_mid[sl(*u)], k_mid[sl(*u)]), 0.0).astype(BF16) for u in units}
    intra = {u: _dot(scores[u], v[sl(*u)]) for u in units}
    updates = {(ci, h): _dot(vt[h * HGRN_DIM:(h + 1) * HGRN_DIM, ci * c:(ci + 1) * c], k_out[sl(ci, h)])
               for ci, h in units}

    states = [state_ref[h] for h in range(HGRN_HEADS)]
    for ci in range(n_chunks):
        for h in range(HGRN_HEADS):
            rs, hs = sl(ci, h)
            o = intra[(ci, h)] + _dot_nt(q_in[rs, hs], states[h].astype(BF16))
            decay = jnp.exp(bc[ci * c + c - 1:ci * c + c, hs])
            states[h] = states[h] * decay + updates[(ci, h)]
            y = o * lax.rsqrt(jnp.mean(o * o, axis=1, keepdims=True) + LN_EPS) * g_ref[...] * gate[rs, hs]
            o_ref[rs, hs] = y.astype(o_ref.dtype)
    for h in range(HGRN_HEADS):
        state_ref[h] = states[h]


def _hgrn(p_main, g_gate, hgrn_lb, norm_g, layer, batch, seq, tc):
    ns = seq // tc
    chunk_tri = np.tril(np.ones((HGRN_CHUNK, HGRN_CHUNK), np.float32))
    tri = jnp.asarray(np.kron(np.eye(tc // HGRN_CHUNK, dtype=np.float32), chunk_tri), BF16)
    col = lambda c0: pl.BlockSpec((tc, HGRN_W), lambda b, s: (b * ns + s, c0 // HGRN_W))
    return pl.pallas_call(
        functools.partial(_hgrn_kernel, layer=layer, tc=tc), grid=(batch, ns),
        in_specs=[col(C_HQ), col(C_HI), col(C_HG),
                  pl.BlockSpec((tc, HGRN_W), lambda b, s: (b * ns + s, 0)),
                  pl.BlockSpec((DEPTH, HGRN_W), lambda b, s: (0, 0)),
                  pl.BlockSpec((1, HGRN_DIM), lambda b, s: (0, 0)),
                  pl.BlockSpec((tc, tc), lambda b, s: (0, 0))],
        out_specs=pl.BlockSpec((tc, HGRN_W), lambda b, s: (b * ns + s, 0)),
        out_shape=jax.ShapeDtypeStruct((batch * seq, HGRN_W), BF16),
        scratch_shapes=[pltpu.VMEM((HGRN_HEADS, HGRN_DIM, HGRN_DIM), F32)],
        compiler_params=_params(("parallel", "arbitrary")), name="hgrn2",
    )(p_main, p_main, p_main, g_gate, hgrn_lb, norm_g, tri)


def _proj_ln_kernel(yf_ref, yd_ref, yh_ref, w_ref, h_ref, g_ref, b_ref, ho_ref, hbo_ref):
    fw = yf_ref.shape[1]
    dw = fw + yd_ref.shape[1]
    mix = _dot(yf_ref[...], w_ref[0:fw]) + _dot(yd_ref[...], w_ref[fw:dw]) + _dot(yh_ref[...], w_ref[dw:])
    out = _layer_norm(ALPHA * h_ref[...] + mix, g_ref[...], b_ref[...])
    ho_ref[...] = out
    hbo_ref[...] = out.astype(BF16)


def _proj_ln(y_fox, y_diff, y_hgrn, w, h, g, b, tm):
    n, d = h.shape
    row = lambda width: pl.BlockSpec((tm, width), lambda i: (i, 0))
    vec = pl.BlockSpec((1, d), lambda i: (0, 0))
    return pl.pallas_call(
        _proj_ln_kernel, grid=(n // tm,),
        in_specs=[row(y_fox.shape[1]), row(y_diff.shape[1]), row(y_hgrn.shape[1]),
                  pl.BlockSpec(w.shape, lambda i: (0, 0)), row(d), vec, vec],
        out_specs=[row(d), row(d)],
        out_shape=[jax.ShapeDtypeStruct((n, d), F32), jax.ShapeDtypeStruct((n, d), BF16)],
        compiler_params=_params(("parallel",)), name="out_proj_ln")(y_fox, y_diff, y_hgrn, w, h, g, b)


def _mem_attn_kernel(hb_ref, h_ref, wq_ref, k_ref, v_ref, wo_ref, g_ref, b_ref, ho_ref):
    q = _dot(hb_ref[...], wq_ref[...])
    outs = []
    for hd in range(MEM_HEADS):
        hs = slice(hd * MEM_DIM, (hd + 1) * MEM_DIM)
        s = _dot_nt(q[:, hs].astype(BF16), k_ref[:, hs])
        p = jnp.exp2(s - jnp.max(s, axis=1, keepdims=True))
        o = _dot(p.astype(BF16), v_ref[:, hs]) / jnp.sum(p, axis=1, keepdims=True)
        outs.append(o.astype(BF16))
    att = jnp.concatenate(outs, axis=1)
    z = ALPHA * h_ref[...] + _dot(att, wo_ref[...])
    ho_ref[...] = _layer_norm(z, g_ref[...], b_ref[...])


def _mem_attn(hb, h, wq, kmem, vmem, wo, g, b, batch, seq, tm):
    n, d = h.shape
    ns = seq // tm
    mem_len = kmem.shape[0] // batch
    row = pl.BlockSpec((tm, d), lambda bi, s: (bi * ns + s, 0))
    full = pl.BlockSpec((d, d), lambda bi, s: (0, 0))
    mem = pl.BlockSpec((mem_len, d), lambda bi, s: (bi, 0))
    vec = pl.BlockSpec((1, d), lambda bi, s: (0, 0))
    return pl.pallas_call(
        _mem_attn_kernel, grid=(batch, ns),
        in_specs=[row, row, full, mem, mem, full, vec, vec], out_specs=row,
        out_shape=jax.ShapeDtypeStruct((n, d), F32),
        compiler_params=_params(("parallel", "arbitrary")), name="mem_attn")(hb, h, wq, kmem, vmem, wo, g, b)


def _top2_sum(a, b, c, d):
    hi1, lo1 = jnp.maximum(a, b), jnp.minimum(a, b)
    hi2, lo2 = jnp.maximum(c, d), jnp.minimum(c, d)
    return jnp.maximum(hi1, hi2) + jnp.maximum(jnp.minimum(hi1, hi2), jnp.maximum(lo1, lo2))


def _router_kernel(h_ref, rw_ref, rb_ref, gate_ref, pick_ref, count_ref):
    xh, xm, xl = _split3(h_ref[...])
    wh, wm, wl = rw_ref[0], rw_ref[1], rw_ref[2]
    logits = (_dot_nt(wh, xh) + _dot_nt(wh, xm) + _dot_nt(wm, xh)
              + _dot_nt(wh, xl) + _dot_nt(wm, xm) + _dot_nt(wl, xh))
    scores = _sigmoid(logits)
    biased = scores + rb_ref[...]
    s_rows = [scores[e:e + 1] for e in range(N_EXPERTS)]
    b_rows = [biased[e:e + 1] for e in range(N_EXPERTS)]
    group_score = [_top2_sum(*b_rows[4 * g:4 * g + 4]) for g in range(N_GROUPS)]
    best = group_score[0]
    best_idx = jnp.zeros_like(best, dtype=jnp.int32)
    for g in range(1, N_GROUPS):
        better = group_score[g] > best
        best_idx = jnp.where(better, g, best_idx)
        best = jnp.where(better, group_score[g], best)
    picked = []
    for e in range(N_EXPERTS):
        g = e // EXPERTS_PER_GROUP
        rank = jnp.zeros_like(best)
        for j in range(4 * g, 4 * g + 4):
            if j == e:
                continue
            ahead = (b_rows[j] > b_rows[e]) | (b_rows[j] == b_rows[e]) if j < e else (b_rows[j] > b_rows[e])
            rank = rank + jnp.where(ahead, 1.0, 0.0)
        picked.append((best_idx == g) & (rank < 2.0))
    denom = jnp.zeros_like(best)
    for e in range(N_EXPERTS):
        denom = denom + jnp.where(picked[e], s_rows[e], 0.0)
    for e in range(N_EXPERTS):
        gate_ref[e:e + 1, :] = jnp.where(picked[e], s_rows[e] / denom, 0.0)
        pick_ref[e:e + 1, :] = jnp.where(picked[e], 1.0, 0.0)

    @pl.when(pl.program_id(0) == 0)
    def _():
        count_ref[...] = jnp.zeros_like(count_ref)

    count_ref[...] += jnp.broadcast_to(jnp.sum(pick_ref[...], axis=1, keepdims=True), count_ref.shape)


def _router(h, rw_parts, rb_col, tm):
    n, d = h.shape
    rows = pl.BlockSpec((N_EXPERTS, tm), lambda i: (0, i))
    return pl.pallas_call(
        _router_kernel, grid=(n // tm,),
        in_specs=[pl.BlockSpec((tm, d), lambda i: (i, 0)),
                  pl.BlockSpec((3, N_EXPERTS, d), lambda i: (0, 0, 0)),
                  pl.BlockSpec((N_EXPERTS, 1), lambda i: (0, 0))],
        out_specs=[rows, rows, pl.BlockSpec((N_EXPERTS, LANES), lambda i: (0, 0))],
        out_shape=[jax.ShapeDtypeStruct((N_EXPERTS, n), F32), jax.ShapeDtypeStruct((N_EXPERTS, n), F32),
                   jax.ShapeDtypeStruct((N_EXPERTS, LANES), F32)],
        compiler_params=_params(("arbitrary",)), name="router")(h, rw_parts, rb_col)


def _moe_pos_kernel(pick_ref, gate_ref, start_ref, upper_ref, lower_ref, pos_ref, gcol_ref, carry_ref, rows_ref):
    @pl.when(pl.program_id(0) == 0)
    def _():
        carry_ref[...] = jnp.zeros_like(carry_ref)

    pick = pick_ref[...]
    pick_b = pick.astype(BF16)
    rank = _dot(pick_b, upper_ref[...]) + carry_ref[:, 0:1]
    slot = start_ref[...] + rank
    nth = _dot(lower_ref[...], pick_b)
    first = pick * jnp.where(nth == 1.0, 1.0, 0.0)
    second = pick * jnp.where(nth == 2.0, 1.0, 0.0)
    pos_ref[0:1, :] = jnp.sum(first * slot, axis=0, keepdims=True).astype(jnp.int32)
    pos_ref[1:2, :] = jnp.sum(second * slot, axis=0, keepdims=True).astype(jnp.int32)
    gate = gate_ref[...]
    rows_ref[...] = jnp.zeros_like(rows_ref)
    rows_ref[0:1, :] = jnp.sum(first * gate, axis=0, keepdims=True)
    rows_ref[1:2, :] = jnp.sum(second * gate, axis=0, keepdims=True)
    gcol_ref[...] = rows_ref[...].T
    carry_ref[...] += jnp.broadcast_to(jnp.sum(pick, axis=1, keepdims=True), carry_ref.shape)


def _moe_pos(picks, gates, starts_col, tm):
    n = picks.shape[1]
    upper = jnp.asarray(np.triu(np.ones((tm, tm), np.float32), 1), BF16)
    lower = jnp.asarray(np.tril(np.ones((N_EXPERTS, N_EXPERTS), np.float32)), BF16)
    rows = pl.BlockSpec((N_EXPERTS, tm), lambda i: (0, i))
    return pl.pallas_call(
        _moe_pos_kernel, grid=(n // tm,),
        in_specs=[rows, rows, pl.BlockSpec((N_EXPERTS, 1), lambda i: (0, 0)),
                  pl.BlockSpec((tm, tm), lambda i: (0, 0)), pl.BlockSpec((N_EXPERTS, N_EXPERTS), lambda i: (0, 0))],
        out_specs=[pl.BlockSpec((2, tm), lambda i: (0, i)), pl.BlockSpec((tm, LANES), lambda i: (i, 0))],
        out_shape=[jax.ShapeDtypeStruct((2, n), jnp.int32), jax.ShapeDtypeStruct((n, LANES), F32)],
        scratch_shapes=[pltpu.VMEM((N_EXPERTS, LANES), F32), pltpu.VMEM((LANES, tm), F32)],
        compiler_params=_params(("arbitrary",)), name="moe_pos")(picks, gates, starts_col, upper, lower)


def _row_copy(src_ref, src_row, dst_ref, dst_row, sem):
    return pltpu.make_async_copy(src_ref.at[pl.ds(src_row, 1)], dst_ref.at[pl.ds(dst_row, 1)], sem)


def _dispatch_kernel(pos1_ref, pos2_ref, tail_ref, used_ref, h_ref, xs_ref, zero_ref, sem, *, tm, tile):
    @pl.when(pl.program_id(0) == 0)
    def _():
        zero_ref[...] = jnp.zeros_like(zero_ref)

        def fill(row0):
            copy = pltpu.make_async_copy(zero_ref, xs_ref.at[pl.ds(pl.multiple_of(row0, tile), tile)], sem.at[2])
            copy.start()
            copy.wait()

        for e in range(N_EXPERTS):
            @pl.when(tail_ref[e] >= 0)
            def _():
                fill(tail_ref[e])

        def fill_unused(t, carry):
            fill(t * tile)
            return carry

        lax.fori_loop(used_ref[0], xs_ref.shape[0] // tile, fill_unused, 0)

    def issue(i, carry):
        for k in range(2):
            r = 2 * i + k
            _row_copy(h_ref, r, xs_ref, pos1_ref[r], sem.at[0]).start(priority=k)
            _row_copy(h_ref, r, xs_ref, pos2_ref[r], sem.at[1]).start(priority=1 - k)
        return carry

    lax.fori_loop(0, tm // 2, issue, 0)

    def drain(r, carry):
        _row_copy(h_ref, 0, xs_ref, 0, sem.at[0]).wait()
        _row_copy(h_ref, 0, xs_ref, 0, sem.at[1]).wait()
        return carry

    lax.fori_loop(0, tm, drain, 0)


def _dispatch(h, pos1, pos2, tails, n_used, n_rows, tm, tile):
    n, d = h.shape
    smem = lambda: pl.BlockSpec((tm,), lambda i: (i,), memory_space=pltpu.SMEM)
    whole_smem = pl.BlockSpec(memory_space=pltpu.SMEM)
    return pl.pallas_call(
        functools.partial(_dispatch_kernel, tm=tm, tile=tile), grid=(n // tm,),
        in_specs=[smem(), smem(), whole_smem, whole_smem, pl.BlockSpec((tm, d), lambda i: (i, 0))],
        out_specs=pl.BlockSpec(memory_space=pl.ANY),
        out_shape=jax.ShapeDtypeStruct((n_rows, d), F32),
        scratch_shapes=[pltpu.VMEM((tile, d), F32), pltpu.SemaphoreType.DMA((3,))],
        compiler_params=_params(("arbitrary",)), name="moe_dispatch")(pos1, pos2, tails, n_used, h)


def _expert_kernel(src_ref, exp_ref, used_ref, xs_ref, w1_ref, w3_ref, w2_ref, ys_ref):
    t = pl.program_id(0)

    @pl.when(t < used_ref[0])
    def _():
        x = xs_ref[...].astype(BF16)
        a = _dot(x, w1_ref[0])
        hid = a * (0.5 + 0.5 * jnp.tanh(0.5 * a)) * _dot(x, w3_ref[0])
        ys_ref[...] = _dot(hid.astype(BF16), w2_ref[0])

    @pl.when(t >= used_ref[0])
    def _():
        ys_ref[...] = jnp.zeros_like(ys_ref)


def _experts(xs, tile_src, tile_exp, n_used, w1, w3, w2, tile):
    n_rows, d = xs.shape
    ff = w1.shape[2]
    grid_spec = pltpu.PrefetchScalarGridSpec(
        num_scalar_prefetch=3, grid=(n_rows // tile,),
        in_specs=[pl.BlockSpec((tile, d), lambda t, src, exp, used: (src[t], 0)),
                  pl.BlockSpec((1, d, ff), lambda t, src, exp, used: (exp[t], 0, 0)),
                  pl.BlockSpec((1, d, ff), lambda t, src, exp, used: (exp[t], 0, 0)),
                  pl.BlockSpec((1, ff, d), lambda t, src, exp, used: (exp[t], 0, 0))],
        out_specs=pl.BlockSpec((tile, d), lambda t, src, exp, used: (t, 0)))
    return pl.pallas_call(
        _expert_kernel, grid_spec=grid_spec, out_shape=jax.ShapeDtypeStruct((n_rows, d), F32),
        compiler_params=_params(("arbitrary",)), name="moe_experts",
    )(tile_src, tile_exp, n_used, xs, w1, w3, w2)


def _combine_kernel(pos1_ref, pos2_ref, ys_ref, h_ref, gcol_ref, g_ref, b_ref, ho_ref, hbo_ref,
                    y1_ref, y2_ref, sem, *, tm):
    def issue(i, carry):
        for k in range(2):
            r = 2 * i + k
            _row_copy(ys_ref, pos1_ref[r], y1_ref, r, sem.at[0]).start(priority=k)
            _row_copy(ys_ref, pos2_ref[r], y2_ref, r, sem.at[1]).start(priority=1 - k)
        return carry

    lax.fori_loop(0, tm // 2, issue, 0)

    def drain(r, carry):
        _row_copy(ys_ref, 0, y1_ref, 0, sem.at[0]).wait()
        _row_copy(ys_ref, 0, y2_ref, 0, sem.at[1]).wait()
        return carry

    lax.fori_loop(0, tm, drain, 0)
    gates = gcol_ref[...]
    moe = gates[:, 0:1] * y1_ref[...] + gates[:, 1:2] * y2_ref[...]
    out = _layer_norm(ALPHA * h_ref[...] + moe, g_ref[...], b_ref[...])
    ho_ref[...] = out
    hbo_ref[...] = out.astype(BF16)


def _combine(ys, pos1, pos2, h, gcol, g, b, tm):
    n, d = h.shape
    smem = lambda: pl.BlockSpec((tm,), lambda i: (i,), memory_space=pltpu.SMEM)
    row = lambda width: pl.BlockSpec((tm, width), lambda i: (i, 0))
    vec = pl.BlockSpec((1, d), lambda i: (0, 0))
    return pl.pallas_call(
        functools.partial(_combine_kernel, tm=tm), grid=(n // tm,),
        in_specs=[smem(), smem(), pl.BlockSpec(memory_space=pl.ANY), row(d), row(LANES), vec, vec],
        out_specs=[row(d), row(d)],
        out_shape=[jax.ShapeDtypeStruct((n, d), F32), jax.ShapeDtypeStruct((n, d), BF16)],
        scratch_shapes=[pltpu.VMEM((tm, d), F32), pltpu.VMEM((tm, d), F32), pltpu.SemaphoreType.DMA((2,))],
        compiler_params=_params(("arbitrary",)), name="moe_combine")(pos1, pos2, ys, h, gcol, g, b)


def _routed_moe(h, rw_parts, rb_col, w1, w3, w2, g, b, tile):
    n, d = h.shape
    gates, picks, counts = _router(h, rw_parts, rb_col, min(512, n))
    count = counts[:, 0].astype(jnp.int32)
    padded = (count + tile - 1) // tile * tile
    ends = jnp.cumsum(padded)
    starts = ends - padded
    n_tiles = (2 * n) // tile + N_EXPERTS
    n_used = (ends[-1] // tile).reshape(1).astype(jnp.int32)
    tile_src = jnp.minimum(jnp.arange(n_tiles, dtype=jnp.int32), n_used - 1)
    tile_exp = jnp.minimum(jnp.sum((tile_src[:, None] * tile >= ends[None, :]).astype(jnp.int32), axis=1),
                           N_EXPERTS - 1)
    tails = jnp.where(padded > 0, ends - tile, -1).astype(jnp.int32)

    pos, gcol = _moe_pos(picks, gates, starts.astype(F32).reshape(N_EXPERTS, 1), min(512, n))
    xs = _dispatch(h, pos[0], pos[1], tails, n_used, n_tiles * tile, min(512, n), tile)
    ys = _experts(xs, tile_src, tile_exp, n_used, w1, w3, w2, tile)
    return _combine(ys, pos[0], pos[1], h, gcol, g, b, min(256, n))


def _pack_in_weights(w_in):
    sizes = (256, 256, 256, FOX_HEADS, 256, 256, 256, HGRN_W, HGRN_W, HGRN_W, HGRN_W)
    offs = np.cumsum((0,) + sizes)
    fq, fk, fv, ff, dq, dk, dv, hq, hf, hi, hg = (w_in[:, offs[i]:offs[i + 1]] for i in range(len(sizes)))
    fox_scale = FOX_DIM ** -0.5 * LOG2E
    diff_scale = DIFF_QK_DIM ** -0.5 * LOG2E
    main = jnp.concatenate([fq * fox_scale, fk, fv, dq * diff_scale, dk, dv, hq, hi, hg], axis=1).astype(BF16)
    ff_pad = jnp.zeros((w_in.shape[0], LANES), w_in.dtype).at[:, :FOX_HEADS].set(ff)
    gate = jnp.concatenate([hf, ff_pad], axis=1).astype(BF16)
    return main, gate


def kernel(x, mem, ln_in_g, ln_in_b, w_in, fox_fb, lam_q1, lam_k1, lam_q2, lam_k2, diff_norm_g, hgrn_lb,
           hgrn_norm_g, w_out, mem_wq, mem_wk, mem_wv, mem_wo, router_w, router_b, w1, w3, w2, ln_g, ln_b):
    batch, seq, d = x.shape
    n = batch * seq
    tm = min(512, seq)
    attn_tile = min(512, seq)
    vec = lambda v: v.reshape(1, -1).astype(F32)

    h, hb = _entry_ln(x.reshape(n, d), vec(ln_in_g), vec(ln_in_b), tm)
    memb = mem.reshape(-1, d).astype(BF16)
    rw_parts = jnp.stack(_split3(router_w.T.astype(F32)))
    rb_col = router_b.reshape(N_EXPERTS, 1).astype(F32)
    hgrn_lb = hgrn_lb.astype(F32)

    for i in range(DEPTH):
        lam_init = 0.8 - 0.6 * math.exp(-0.3 * i)
        w_main, w_gate = _pack_in_weights(w_in[i])
        p_main, g_gate = _in_proj(hb, w_main, w_gate, tm)

        q_aug, k_aug, fvt, dvt = _fox_prep(p_main, g_gate, fox_fb[i].astype(F32), batch, seq, min(512, seq))
        y_fox = _fox_attn(q_aug, k_aug, fvt, attn_tile)

        lam_pack = jnp.zeros((8, LANES), F32)
        for r, v in enumerate((lam_q1[i], lam_k1[i], lam_q2[i], lam_k2[i])):
            lam_pack = lam_pack.at[r, :DIFF_QK_DIM].set(v.astype(F32))
        diff_g = jnp.tile(vec(diff_norm_g[i]), (1, DIFF_HEADS))
        y_diff = _diff_attn(p_main, dvt, lam_pack, diff_g, lam_init, batch, seq, attn_tile)

        y_hgrn = _hgrn(p_main, g_gate, hgrn_lb, vec(hgrn_norm_g[i]), i, batch, seq, min(256, seq))
        h, hb = _proj_ln(y_fox, y_diff, y_hgrn, w_out[i].astype(BF16), h, vec(ln_g[i, 0]), vec(ln_b[i, 0]), tm)

        kmem = _matmul(memb, mem_wk[i].astype(BF16), BF16, memb.shape[0], 512, "mem_k")
        vmem = _matmul(memb, mem_wv[i].astype(BF16), BF16, memb.shape[0], 512, "mem_v")
        wq = (mem_wq[i] * (MEM_DIM ** -0.5 * LOG2E)).astype(BF16)
        h = _mem_attn(hb, h, wq, kmem, vmem, mem_wo[i].astype(BF16), vec(ln_g[i, 1]), vec(ln_b[i, 1]),
                          batch, seq, tm)

        h, hb = _routed_moe(h, rw_parts, rb_col, w1[i].astype(BF16), w3[i].astype(BF16), w2[i].astype(BF16),
                            vec(ln_g[i, 2]), vec(ln_b[i, 2]), min(512, n // 8))
    return h.reshape(batch, seq, d)
```

```python
import functools
import math

import numpy as np
import jax
import jax.numpy as jnp
from jax import lax
from jax.experimental import pallas as pl
from jax.experimental.pallas import tpu as pltpu

F32 = jnp.float32
BF16 = jnp.bfloat16

D_MODEL = 1024
DEPTH = 2
FOX_DIM = 64
FOX_HEADS = 4
DIFF_QK_DIM = 32
DIFF_V_DIM = 64
DIFF_HEADS = 4
HGRN_DIM = 128
HGRN_HEADS = 4
HGRN_W = HGRN_DIM * HGRN_HEADS
HGRN_CHUNK = 64
MEM_HEADS = 4
MEM_DIM = D_MODEL // MEM_HEADS
N_EXPERTS = 16
N_GROUPS = 4
EXPERTS_PER_GROUP = 4
EXPERT_FF = D_MODEL // 2
ALPHA = (2 * DEPTH) ** 0.25
LN_EPS = 1e-5
LOG2E = 1.4426950408889634
NEG_INIT = -1e30
LANES = 128
V_ROWS = 80
VMEM_LIMIT = 56 * 1024 * 1024

C_FQ, C_FK, C_FV, C_DQ, C_DK, C_DV, C_HQ, C_HI, C_HG = (0, 256, 512, 768, 1024, 1280, 1536, 2048, 2560)
MAIN_W = 3072
GATE_W = HGRN_W + LANES


def _params(sem, vmem=VMEM_LIMIT, flags=None):
    return pltpu.CompilerParams(dimension_semantics=sem, vmem_limit_bytes=vmem, flags=flags)


def _split3(x):
    hi = x.astype(BF16)
    r = x - hi.astype(F32)
    mid = r.astype(BF16)
    lo = (r - mid.astype(F32)).astype(BF16)
    return hi, mid, lo


def _dot(a, b):
    return jnp.dot(a, b, preferred_element_type=F32)


def _dot_nt(a, b):
    return lax.dot_general(a, b, (((1,), (1,)), ((), ())), preferred_element_type=F32)


def _sigmoid(x):
    return 1.0 / (1.0 + jnp.exp(-x))


def _log_sigmoid(x):
    return jnp.minimum(x, 0.0) - jnp.log(1.0 + jnp.exp(-jnp.abs(x)))


def _layer_norm(x, g, b):
    mu = jnp.mean(x, axis=-1, keepdims=True)
    xc = x - mu
    var = jnp.mean(xc * xc, axis=-1, keepdims=True)
    return xc * lax.rsqrt(var + LN_EPS) * g + b


def _ln_kernel(x_ref, g_ref, b_ref, h_ref, hb_ref):
    y = _layer_norm(x_ref[...], g_ref[...], b_ref[...])
    h_ref[...] = y
    hb_ref[...] = y.astype(BF16)


def _entry_ln(x, g, b, tm):
    n, d = x.shape
    row = pl.BlockSpec((tm, d), lambda i: (i, 0))
    vec = pl.BlockSpec((1, d), lambda i: (0, 0))
    return pl.pallas_call(
        _ln_kernel, grid=(n // tm,), in_specs=[row, vec, vec], out_specs=[row, row],
        out_shape=[jax.ShapeDtypeStruct((n, d), F32), jax.ShapeDtypeStruct((n, d), BF16)],
        compiler_params=_params(("parallel",)), name="entry_ln")(x, g, b)


def _mm_kernel(x_ref, w_ref, o_ref):
    o_ref[...] = _dot(x_ref[...], w_ref[...]).astype(o_ref.dtype)


def _matmul(x, w, out_dtype, tm, tn, name):
    n, k = x.shape
    m = w.shape[1]
    return pl.pallas_call(
        _mm_kernel, grid=(n // tm, m // tn),
        in_specs=[pl.BlockSpec((tm, k), lambda i, j: (i, 0)), pl.BlockSpec((k, tn), lambda i, j: (0, j))],
        out_specs=pl.BlockSpec((tm, tn), lambda i, j: (i, j)),
        out_shape=jax.ShapeDtypeStruct((n, m), out_dtype),
        compiler_params=_params(("parallel", "arbitrary")), name=name)(x, w)


IN_PROJ_CHUNK = 512


def _in_proj_kernel(x_ref, wm_ref, wg_ref, pm_ref, pg_ref):
    x = x_ref[...]
    for c0 in range(0, MAIN_W, IN_PROJ_CHUNK):
        cs = slice(c0, c0 + IN_PROJ_CHUNK)
        pm_ref[:, cs] = _dot(x, wm_ref[:, cs]).astype(BF16)
    pg_ref[...] = _dot(x, wg_ref[...])


def _in_proj(hb, w_main, w_gate, tm):
    n, d = hb.shape
    return pl.pallas_call(
        _in_proj_kernel, grid=(n // tm,),
        in_specs=[pl.BlockSpec((tm, d), lambda i: (i, 0)),
                  pl.BlockSpec((d, MAIN_W), lambda i: (0, 0)), pl.BlockSpec((d, GATE_W), lambda i: (0, 0))],
        out_specs=[pl.BlockSpec((tm, MAIN_W), lambda i: (i, 0)), pl.BlockSpec((tm, GATE_W), lambda i: (i, 0))],
        out_shape=[jax.ShapeDtypeStruct((n, MAIN_W), BF16), jax.ShapeDtypeStruct((n, GATE_W), F32)],
        compiler_params=_params(("parallel",)), name="in_proj")(hb, w_main, w_gate)


def _value_rows(v_ref, vt_ref, heads, dim):
    ts = v_ref.shape[0]
    vt = v_ref[...].astype(F32).T.astype(BF16)
    tail_row = lax.broadcasted_iota(jnp.int32, (V_ROWS - dim, ts), 0)
    tail = jnp.where(tail_row == 0, 1.0, 0.0).astype(BF16)
    for h in range(heads):
        vt_ref[0, h, 0:dim, :] = vt[h * dim:(h + 1) * dim]
        vt_ref[0, h, dim:V_ROWS, :] = tail


def _fox_prep_kernel(fq_ref, fk_ref, fv_ref, dv_ref, ff_ref, fb_ref, tri_ref, sel_ref, eq_ref, ek_ref, cq_ref,
                     ck_ref, qa_ref, ka_ref, fvt_ref, dvt_ref, carry_ref):
    @pl.when(pl.program_id(1) == 0)
    def _():
        carry_ref[...] = jnp.zeros_like(carry_ref)

    _value_rows(fv_ref, fvt_ref, FOX_HEADS, FOX_DIM)
    _value_rows(dv_ref, dvt_ref, DIFF_HEADS, DIFF_V_DIM)

    logf = _log_sigmoid(ff_ref[...] + fb_ref[...])
    tri = tri_ref[...]
    hi, mid, lo = _split3(logf)
    cum = _dot(tri, hi) + _dot(tri, mid) + _dot(tri, lo) + carry_ref[...]
    ts = cum.shape[0]
    carry_ref[...] = cum[ts - 1:ts, :]
    parts = jnp.concatenate(_split3(cum * LOG2E), axis=1)
    fq = fq_ref[...]
    fk = fk_ref[...]
    for h in range(FOX_HEADS):
        pair = slice((h // 2) * LANES, (h // 2 + 1) * LANES)
        sel = sel_ref[h % 2]
        qa = _dot(fq[:, pair], sel) + _dot(parts, eq_ref[h]) + cq_ref[...]
        ka = _dot(fk[:, pair], sel) + _dot(parts, ek_ref[h]) + ck_ref[...]
        qa_ref[0, h] = qa.astype(BF16)
        ka_ref[0, h] = ka.astype(BF16)


def _fox_prep_consts():
    sel = np.zeros((2, LANES, LANES), np.float32)
    for half in range(2):
        for d in range(FOX_DIM):
            sel[half, half * FOX_DIM + d, d] = 1.0
    eq = np.zeros((FOX_HEADS, 3 * LANES, LANES), np.float32)
    ek = np.zeros((FOX_HEADS, 3 * LANES, LANES), np.float32)
    for h in range(FOX_HEADS):
        for p in range(3):
            eq[h, p * LANES + h, FOX_DIM + p] = 1.0
            ek[h, p * LANES + h, FOX_DIM + 3 + p] = -1.0
    cq = np.zeros((1, LANES), np.float32)
    ck = np.zeros((1, LANES), np.float32)
    cq[0, FOX_DIM + 3:FOX_DIM + 6] = 1.0
    ck[0, FOX_DIM:FOX_DIM + 3] = 1.0
    return (jnp.asarray(sel, BF16), jnp.asarray(eq, BF16), jnp.asarray(ek, BF16), jnp.asarray(cq), jnp.asarray(ck))


def _fox_prep(p_main, g_gate, fb, batch, seq, ts):
    ns = seq // ts
    tri = jnp.asarray(np.tril(np.ones((ts, ts), np.float32)), BF16)
    sel, eq, ek, cq, ck = _fox_prep_consts()
    fb_pad = jnp.zeros((1, LANES), F32).at[0, :FOX_HEADS].set(fb)
    const2 = lambda shape: pl.BlockSpec(shape, lambda b, s: (0,) * len(shape))
    out_spec = pl.BlockSpec((1, FOX_HEADS, ts, LANES), lambda b, s: (b, 0, s, 0))
    out_sds = jax.ShapeDtypeStruct((batch, FOX_HEADS, seq, LANES), BF16)
    vt_spec = pl.BlockSpec((1, FOX_HEADS, V_ROWS, ts), lambda b, s: (b, 0, 0, s))
    vt_sds = jax.ShapeDtypeStruct((batch, FOX_HEADS, V_ROWS, seq), BF16)
    col = lambda c0: pl.BlockSpec((ts, 256), lambda b, s: (b * ns + s, c0 // 256))
    return pl.pallas_call(
        _fox_prep_kernel, grid=(batch, ns),
        in_specs=[col(C_FQ), col(C_FK), col(C_FV), col(C_DV),
                  pl.BlockSpec((ts, LANES), lambda b, s: (b * ns + s, HGRN_W // LANES)),
                  const2((1, LANES)), const2((ts, ts)), const2((2, LANES, LANES)),
                  const2((FOX_HEADS, 3 * LANES, LANES)), const2((FOX_HEADS, 3 * LANES, LANES)),
                  const2((1, LANES)), const2((1, LANES))],
        out_specs=[out_spec, out_spec, vt_spec, vt_spec], out_shape=[out_sds, out_sds, vt_sds, vt_sds],
        scratch_shapes=[pltpu.VMEM((1, LANES), F32)],
        compiler_params=_params(("arbitrary", "arbitrary")), name="fox_prep",
    )(p_main, p_main, p_main, p_main, g_gate, fb_pad, tri, sel, eq, ek, cq, ck)


def _flash_sweep(qi, n_streams, score_fn, value_fn, visible, sa_ref, sb_ref, m_ref, acc_ref):
    m_ref[...] = jnp.full(m_ref.shape, NEG_INIT, F32)
    acc_ref[...] = jnp.zeros(acc_ref.shape, F32)

    def consume(src_ref, st, j, masked):
        s = src_ref[st]
        if masked:
            s = jnp.where(visible, s, -jnp.inf)
        m = m_ref[st]
        m_new = jnp.maximum(m, jnp.max(s, axis=0, keepdims=True))
        p = jnp.exp2(s - m_new).astype(BF16)
        acc_ref[st] = jnp.exp2(m - m_new) * acc_ref[st] + _dot(value_fn(st, j), p)
        m_ref[st] = m_new

    def stage(dst_ref, src_ref, j_next, j_cur):
        for st in range(n_streams):
            dst_ref[st] = score_fn(st, j_next)
            consume(src_ref, st, j_cur, False)

    for st in range(n_streams):
        sa_ref[st] = score_fn(st, 0)

    def pair(p, carry):
        stage(sb_ref, sa_ref, 2 * p + 1, 2 * p)
        stage(sa_ref, sb_ref, 2 * p + 2, 2 * p + 1)
        return carry

    lax.fori_loop(0, qi // 2, pair, 0)

    @pl.when(qi % 2 == 0)
    def _():
        for st in range(n_streams):
            consume(sa_ref, st, qi, True)

    @pl.when(qi % 2 == 1)
    def _():
        stage(sb_ref, sa_ref, qi, qi - 1)
        for st in range(n_streams):
            consume(sb_ref, st, qi, True)


def _flash_scratch(n_streams, tile):
    return [pltpu.VMEM((n_streams, tile, tile), F32), pltpu.VMEM((n_streams, tile, tile), F32),
            pltpu.VMEM((n_streams, 1, tile), F32), pltpu.VMEM((n_streams, V_ROWS, tile), F32)]


def _finish(acc, dim):
    return acc[:dim] / acc[dim:dim + 1]


def _transposed(o, tile):
    dim = o.shape[0]
    padded = jnp.concatenate([o, jnp.zeros((LANES - dim, tile), F32)], axis=0)
    return padded.T[:, :dim]


def _key_block(j, tile):
    return pl.ds(pl.multiple_of(j * tile, tile), tile)


def _fox_attn_kernel(q_ref, k_ref, vt_ref, o_ref, qt_ref, sa_ref, sb_ref, m_ref, acc_ref, *, tile):
    qi = pl.program_id(1)
    for h in range(FOX_HEADS):
        qt_ref[h] = q_ref[0, h].astype(F32).T.astype(BF16)
    row = lax.broadcasted_iota(jnp.int32, (tile, tile), 0)
    col = lax.broadcasted_iota(jnp.int32, (tile, tile), 1)
    visible = row <= col
    _flash_sweep(qi, FOX_HEADS,
                 lambda h, j: _dot(k_ref[0, h, _key_block(j, tile), :], qt_ref[h]),
                 lambda h, j: vt_ref[0, h, :, _key_block(j, tile)],
                 visible, sa_ref, sb_ref, m_ref, acc_ref)
    o = jnp.concatenate([_finish(acc_ref[h], FOX_DIM) for h in range(FOX_HEADS)], axis=0)
    o_ref[...] = o.T.astype(o_ref.dtype)


def _fox_attn(q_aug, k_aug, vt_aug, tile):
    batch, heads, seq, _ = q_aug.shape
    return pl.pallas_call(
        functools.partial(_fox_attn_kernel, tile=tile), grid=(batch, seq // tile),
        in_specs=[pl.BlockSpec((1, heads, tile, LANES), lambda b, i: (b, 0, i, 0)),
                  pl.BlockSpec((1, heads, seq, LANES), lambda b, i: (b, 0, 0, 0)),
                  pl.BlockSpec((1, heads, V_ROWS, seq), lambda b, i: (b, 0, 0, 0))],
        out_specs=pl.BlockSpec((tile, heads * FOX_DIM), lambda b, i: (b * (seq // tile) + i, 0)),
        out_shape=jax.ShapeDtypeStruct((batch * seq, heads * FOX_DIM), BF16),
        scratch_shapes=[pltpu.VMEM((heads, LANES, tile), BF16)] + _flash_scratch(heads, tile),
        compiler_params=_params(("parallel", "arbitrary")), name="fox_attn")(q_aug, k_aug, vt_aug)


def _diff_attn_kernel(q_ref, k_ref, vt_ref, lam_ref, g_ref, o_ref, q12_ref, sa_ref, sb_ref, m_ref, acc_ref,
                      *, tile, lam_init):
    qi = pl.program_id(1)
    qt = q_ref[...].astype(F32).T
    feat = lax.broadcasted_iota(jnp.int32, qt.shape, 0)
    for st in range(2 * DIFF_HEADS):
        keep = (feat >= DIFF_QK_DIM * st) & (feat < DIFF_QK_DIM * (st + 1))
        q12_ref[st] = jnp.where(keep, qt, 0.0).astype(BF16)
    row = lax.broadcasted_iota(jnp.int32, (tile, tile), 0)
    col = lax.broadcasted_iota(jnp.int32, (tile, tile), 1)
    visible = (row // HGRN_CHUNK) <= (col // HGRN_CHUNK)
    _flash_sweep(qi, 2 * DIFF_HEADS,
                 lambda st, j: _dot(k_ref[_key_block(j, tile), :], q12_ref[st]),
                 lambda st, j: vt_ref[0, st // 2, :, _key_block(j, tile)],
                 visible, sa_ref, sb_ref, m_ref, acc_ref)

    lp = lam_ref[...]
    lam = (jnp.exp(jnp.sum(lp[0:1] * lp[1:2], axis=1, keepdims=True))
           - jnp.exp(jnp.sum(lp[2:3] * lp[3:4], axis=1, keepdims=True)) + lam_init)
    outs = []
    for h in range(DIFF_HEADS):
        o = _finish(acc_ref[2 * h], DIFF_V_DIM) - lam * _finish(acc_ref[2 * h + 1], DIFF_V_DIM)
        outs.append(o * lax.rsqrt(jnp.mean(o * o, axis=0, keepdims=True) + LN_EPS))
    y = jnp.concatenate(outs, axis=0).T * g_ref[...] * (1.0 - lam_init)
    o_ref[...] = y.astype(o_ref.dtype)


def _diff_attn(p_main, dvt_aug, lam_pack, norm_g, lam_init, batch, seq, tile):
    heads = DIFF_HEADS
    width = heads * DIFF_V_DIM
    nq = seq // tile
    return pl.pallas_call(
        functools.partial(_diff_attn_kernel, tile=tile, lam_init=lam_init), grid=(batch, nq),
        in_specs=[pl.BlockSpec((tile, width), lambda b, i: (b * nq + i, C_DQ // width)),
                  pl.BlockSpec((seq, width), lambda b, i: (b, C_DK // width)),
                  pl.BlockSpec((1, heads, V_ROWS, seq), lambda b, i: (b, 0, 0, 0)),
                  pl.BlockSpec((8, LANES), lambda b, i: (0, 0)),
                  pl.BlockSpec((1, width), lambda b, i: (0, 0))],
        out_specs=pl.BlockSpec((tile, width), lambda b, i: (b * nq + i, 0)),
        out_shape=jax.ShapeDtypeStruct((batch * seq, width), BF16),
        scratch_shapes=[pltpu.VMEM((2 * heads, width, tile), BF16)] + _flash_scratch(2 * heads, tile),
        compiler_params=_params(("parallel", "arbitrary")), name="diff_attn",
    )(p_main, p_main, dvt_aug, lam_pack, norm_g)


def _hgrn_kernel(hq_ref, hi_ref, hg_ref, hf_ref, lb_ref, g_ref, tri_ref, o_ref, state_ref, *, layer, tc):
    @pl.when(pl.program_id(1) == 0)
    def _():
        state_ref[...] = jnp.zeros_like(state_ref)

    lb_all = lb_ref[...]
    e = jnp.exp(lb_all - jnp.max(lb_all, axis=0, keepdims=True))
    prob = e / jnp.sum(e, axis=0, keepdims=True)
    lb_row = jnp.zeros((1, HGRN_W), F32)
    for j in range(1, layer + 1):
        lb_row = lb_row + prob[j:j + 1]
    lb_row = jnp.maximum(lb_row, 0.0)

    c = HGRN_CHUNK
    n_chunks = tc // c
    half = c // 2 - 1
    r_i = lax.broadcasted_iota(jnp.int32, (c, c), 0)
    c_i = lax.broadcasted_iota(jnp.int32, (c, c), 1)
    causal = r_i >= c_i

    log_lb = jnp.log(lb_row)
    log_1m = jnp.log(1.0 - lb_row)
    z = hf_ref[...]
    ez = jnp.exp(-jnp.abs(z))
    b_term = log_1m + jnp.minimum(z, 0.0) - jnp.log(1.0 + ez)
    logf = jnp.maximum(log_lb, b_term) + jnp.log(1.0 + jnp.exp(-jnp.abs(log_lb - b_term)))
    kk = (1.0 - lb_row) * jnp.where(z >= 0.0, ez, 1.0) / (1.0 + ez)
    xq = hq_ref[...].astype(F32)
    q = xq * (0.5 + 0.5 * jnp.tanh(0.5 * xq))
    xg = hg_ref[...].astype(F32)
    gate = xg * (0.5 + 0.5 * jnp.tanh(0.5 * xg))
    v = hi_ref[...]
    vt = v.astype(F32).T.astype(BF16)

    tri = tri_ref[...]
    g_hi, g_mid, g_lo = _split3(logf)
    bc = _dot(tri, g_hi) + _dot(tri, g_mid) + _dot(tri, g_lo)

    rows = lambda r: jnp.concatenate(
        [jnp.broadcast_to(bc[ci * c + r:ci * c + r + 1], (c, HGRN_W)) for ci in range(n_chunks)], axis=0)
    ref_full = rows(half)
    last_full = rows(c - 1)
    q_mid = q * jnp.exp(bc - ref_full)
    k_mid = kk * jnp.exp(ref_full - bc)
    q_in = (q_mid * jnp.exp(ref_full)).astype(BF16)
    k_out = (k_mid * jnp.exp(last_full - ref_full)).astype(BF16)
    q_mid = q_mid.astype(BF16)
    k_mid = k_mid.astype(BF16)

    units = [(ci, h) for ci in range(n_chunks) for h in range(HGRN_HEADS)]
    sl = lambda ci, h: (slice(ci * c, (ci + 1) * c), slice(h * HGRN_DIM, (h + 1) * HGRN_DIM))
    scores = {u: jnp.where(causal, _dot_nt(q_mid[sl(*u)], k_mid[sl(*u)]), 0.0).astype(BF16) for u in units}
    intra = {u: _dot(scores[u], v[sl(*u)]) for u in units}
    updates = {(ci, h): _dot(vt[h * HGRN_DIM:(h + 1) * HGRN_DIM, ci * c:(ci + 1) * c], k_out[sl(ci, h)])
               for ci, h in units}

    states = [state_ref[h] for h in range(HGRN_HEADS)]
    for ci in range(n_chunks):
        for h in range(HGRN_HEADS):
            rs, hs = sl(ci, h)
            o = intra[(ci, h)] + _dot_nt(q_in[rs, hs], states[h].astype(BF16))
            decay = jnp.exp(bc[ci * c + c - 1:ci * c + c, hs])
            states[h] = states[h] * decay + updates[(ci, h)]
            y = o * lax.rsqrt(jnp.mean(o * o, axis=1, keepdims=True) + LN_EPS) * g_ref[...] * gate[rs, hs]
            o_ref[rs, hs] = y.astype(o_ref.dtype)
    for h in range(HGRN_HEADS):
        state_ref[h] = states[h]


def _hgrn(p_main, g_gate, hgrn_lb, norm_g, layer, batch, seq, tc):
    ns = seq // tc
    chunk_tri = np.tril(np.ones((HGRN_CHUNK, HGRN_CHUNK), np.float32))
    tri = jnp.asarray(np.kron(np.eye(tc // HGRN_CHUNK, dtype=np.float32), chunk_tri), BF16)
    col = lambda c0: pl.BlockSpec((tc, HGRN_W), lambda b, s: (b * ns + s, c0 // HGRN_W))
    return pl.pallas_call(
        functools.partial(_hgrn_kernel, layer=layer, tc=tc), grid=(batch, ns),
        in_specs=[col(C_HQ), col(C_HI), col(C_HG),
                  pl.BlockSpec((tc, HGRN_W), lambda b, s: (b * ns + s, 0)),
                  pl.BlockSpec((DEPTH, HGRN_W), lambda b, s: (0, 0)),
                  pl.BlockSpec((1, HGRN_DIM), lambda b, s: (0, 0)),
                  pl.BlockSpec((tc, tc), lambda b, s: (0, 0))],
        out_specs=pl.BlockSpec((tc, HGRN_W), lambda b, s: (b * ns + s, 0)),
        out_shape=jax.ShapeDtypeStruct((batch * seq, HGRN_W), BF16),
        scratch_shapes=[pltpu.VMEM((HGRN_HEADS, HGRN_DIM, HGRN_DIM), F32)],
        compiler_params=_params(("parallel", "arbitrary")), name="hgrn2",
    )(p_main, p_main, p_main, g_gate, hgrn_lb, norm_g, tri)


def _proj_ln_kernel(yf_ref, yd_ref, yh_ref, w_ref, h_ref, g_ref, b_ref, ho_ref, hbo_ref):
    fw = yf_ref.shape[1]
    dw = fw + yd_ref.shape[1]
    mix = _dot(yf_ref[...], w_ref[0:fw]) + _dot(yd_ref[...], w_ref[fw:dw]) + _dot(yh_ref[...], w_ref[dw:])
    out = _layer_norm(ALPHA * h_ref[...] + mix, g_ref[...], b_ref[...])
    ho_ref[...] = out
    hbo_ref[...] = out.astype(BF16)


def _proj_ln(y_fox, y_diff, y_hgrn, w, h, g, b, tm):
    n, d = h.shape
    row = lambda width: pl.BlockSpec((tm, width), lambda i: (i, 0))
    vec = pl.BlockSpec((1, d), lambda i: (0, 0))
    return pl.pallas_call(
        _proj_ln_kernel, grid=(n // tm,),
        in_specs=[row(y_fox.shape[1]), row(y_diff.shape[1]), row(y_hgrn.shape[1]),
                  pl.BlockSpec(w.shape, lambda i: (0, 0)), row(d), vec, vec],
        out_specs=[row(d), row(d)],
        out_shape=[jax.ShapeDtypeStruct((n, d), F32), jax.ShapeDtypeStruct((n, d), BF16)],
        compiler_params=_params(("parallel",)), name="out_proj_ln")(y_fox, y_diff, y_hgrn, w, h, g, b)


def _mem_attn_kernel(hb_ref, h_ref, wq_ref, k_ref, v_ref, wo_ref, g_ref, b_ref, ho_ref):
    q = _dot(hb_ref[...], wq_ref[...])
    outs = []
    for hd in range(MEM_HEADS):
        hs = slice(hd * MEM_DIM, (hd + 1) * MEM_DIM)
        s = _dot_nt(q[:, hs].astype(BF16), k_ref[:, hs])
        p = jnp.exp2(s - jnp.max(s, axis=1, keepdims=True))
        o = _dot(p.astype(BF16), v_ref[:, hs]) / jnp.sum(p, axis=1, keepdims=True)
        outs.append(o.astype(BF16))
    att = jnp.concatenate(outs, axis=1)
    z = ALPHA * h_ref[...] + _dot(att, wo_ref[...])
    ho_ref[...] = _layer_norm(z, g_ref[...], b_ref[...])


def _mem_attn(hb, h, wq, kmem, vmem, wo, g, b, batch, seq, tm):
    n, d = h.shape
    ns = seq // tm
    mem_len = kmem.shape[0] // batch
    row = pl.BlockSpec((tm, d), lambda bi, s: (bi * ns + s, 0))
    full = pl.BlockSpec((d, d), lambda bi, s: (0, 0))
    mem = pl.BlockSpec((mem_len, d), lambda bi, s: (bi, 0))
    vec = pl.BlockSpec((1, d), lambda bi, s: (0, 0))
    return pl.pallas_call(
        _mem_attn_kernel, grid=(batch, ns),
        in_specs=[row, row, full, mem, mem, full, vec, vec], out_specs=row,
        out_shape=jax.ShapeDtypeStruct((n, d), F32),
        compiler_params=_params(("parallel", "arbitrary")), name="mem_attn")(hb, h, wq, kmem, vmem, wo, g, b)


def _top2_sum(a, b, c, d):
    hi1, lo1 = jnp.maximum(a, b), jnp.minimum(a, b)
    hi2, lo2 = jnp.maximum(c, d), jnp.minimum(c, d)
    return jnp.maximum(hi1, hi2) + jnp.maximum(jnp.minimum(hi1, hi2), jnp.maximum(lo1, lo2))


def _router_kernel(h_ref, rw_ref, rb_ref, gate_ref, pick_ref, count_ref):
    xh, xm, xl = _split3(h_ref[...])
    wh, wm, wl = rw_ref[0], rw_ref[1], rw_ref[2]
    logits = (_dot_nt(wh, xh) + _dot_nt(wh, xm) + _dot_nt(wm, xh)
              + _dot_nt(wh, xl) + _dot_nt(wm, xm) + _dot_nt(wl, xh))
    scores = _sigmoid(logits)
    biased = scores + rb_ref[...]
    s_rows = [scores[e:e + 1] for e in range(N_EXPERTS)]
    b_rows = [biased[e:e + 1] for e in range(N_EXPERTS)]
    group_score = [_top2_sum(*b_rows[4 * g:4 * g + 4]) for g in range(N_GROUPS)]
    best = group_score[0]
    best_idx = jnp.zeros_like(best, dtype=jnp.int32)
    for g in range(1, N_GROUPS):
        better = group_score[g] > best
        best_idx = jnp.where(better, g, best_idx)
        best = jnp.where(better, group_score[g], best)
    picked = []
    for e in range(N_EXPERTS):
        g = e // EXPERTS_PER_GROUP
        rank = jnp.zeros_like(best)
        for j in range(4 * g, 4 * g + 4):
            if j == e:
                continue
            ahead = (b_rows[j] > b_rows[e]) | (b_rows[j] == b_rows[e]) if j < e else (b_rows[j] > b_rows[e])
            rank = rank + jnp.where(ahead, 1.0, 0.0)
        picked.append((best_idx == g) & (rank < 2.0))
    denom = jnp.zeros_like(best)
    for e in range(N_EXPERTS):
        denom = denom + jnp.where(picked[e], s_rows[e], 0.0)
    gate_ref[...] = jnp.zeros_like(gate_ref)
    pick_ref[...] = jnp.zeros_like(pick_ref)
    for slot in range(EXPERTS_PER_GROUP):
        gate = jnp.zeros_like(best)
        for g in range(N_GROUPS):
            e = g * EXPERTS_PER_GROUP + slot
            gate = gate + jnp.where(picked[e], s_rows[e] / denom, 0.0)
        gate_ref[slot:slot + 1, :] = gate
    for g in range(N_GROUPS):
        pick_ref[g:g + 1, :] = jnp.where(best_idx == g, 1.0, 0.0)

    @pl.when(pl.program_id(0) == 0)
    def _():
        count_ref[...] = jnp.zeros_like(count_ref)

    count_ref[...] += jnp.broadcast_to(jnp.sum(pick_ref[...], axis=1, keepdims=True), count_ref.shape)


ROUTE_ROWS = 8


def _router(h, rw_parts, rb_col, tm):
    n, d = h.shape
    rows = pl.BlockSpec((ROUTE_ROWS, tm), lambda i: (0, i))
    return pl.pallas_call(
        _router_kernel, grid=(n // tm,),
        in_specs=[pl.BlockSpec((tm, d), lambda i: (i, 0)),
                  pl.BlockSpec((3, N_EXPERTS, d), lambda i: (0, 0, 0)),
                  pl.BlockSpec((N_EXPERTS, 1), lambda i: (0, 0))],
        out_specs=[rows, rows, pl.BlockSpec((ROUTE_ROWS, LANES), lambda i: (0, 0))],
        out_shape=[jax.ShapeDtypeStruct((ROUTE_ROWS, n), F32), jax.ShapeDtypeStruct((ROUTE_ROWS, n), F32),
                   jax.ShapeDtypeStruct((ROUTE_ROWS, LANES), F32)],
        compiler_params=_params(("arbitrary",)), name="router")(h, rw_parts, rb_col)


def _moe_pos_kernel(pick_ref, gate_ref, start_ref, upper_ref, pos_ref, gcol_ref, carry_ref, rows_ref):
    @pl.when(pl.program_id(0) == 0)
    def _():
        carry_ref[...] = jnp.zeros_like(carry_ref)

    pick = pick_ref[...]
    rank = _dot(pick.astype(BF16), upper_ref[...]) + carry_ref[:, 0:1]
    slot = start_ref[...] + rank
    pos_ref[...] = jnp.sum(pick * slot, axis=0, keepdims=True).astype(jnp.int32)
    rows_ref[...] = jnp.zeros_like(rows_ref)
    rows_ref[0:ROUTE_ROWS, :] = gate_ref[...]
    gcol_ref[...] = rows_ref[...].T
    carry_ref[...] += jnp.broadcast_to(jnp.sum(pick, axis=1, keepdims=True), carry_ref.shape)


def _moe_pos(picks, gates, starts_col, tm):
    n = picks.shape[1]
    upper = jnp.asarray(np.triu(np.ones((tm, tm), np.float32), 1), BF16)
    rows = pl.BlockSpec((ROUTE_ROWS, tm), lambda i: (0, i))
    return pl.pallas_call(
        _moe_pos_kernel, grid=(n // tm,),
        in_specs=[rows, rows, pl.BlockSpec((ROUTE_ROWS, 1), lambda i: (0, 0)),
                  pl.BlockSpec((tm, tm), lambda i: (0, 0))],
        out_specs=[pl.BlockSpec((1, tm), lambda i: (0, i)), pl.BlockSpec((tm, LANES), lambda i: (i, 0))],
        out_shape=[jax.ShapeDtypeStruct((1, n), jnp.int32), jax.ShapeDtypeStruct((n, LANES), F32)],
        scratch_shapes=[pltpu.VMEM((ROUTE_ROWS, LANES), F32), pltpu.VMEM((LANES, tm), F32)],
        compiler_params=_params(("arbitrary",)), name="moe_pos")(picks, gates, starts_col, upper)


def _row_copy(src_ref, src_row, dst_ref, dst_row, sem):
    return pltpu.make_async_copy(src_ref.at[pl.ds(src_row, 1)], dst_ref.at[pl.ds(dst_row, 1)], sem)


DMA_WAIT_UNROLL = 8


def _wait_rows(src_ref, dst_ref, sem, n_rows):
    def drain(i, carry):
        _row_copy(src_ref, 0, dst_ref, 0, sem).wait()
        return carry

    lax.fori_loop(0, n_rows, drain, 0, unroll=DMA_WAIT_UNROLL)


def _dispatch_kernel(pos_ref, tail_ref, used_ref, h_ref, gcol_ref, xs_ref, cat_ref, zero_ref, sem, *, tm, tile):
    i = pl.program_id(0)
    d = h_ref.shape[1]

    @pl.when(i == 0)
    def _():
        zero_ref[...] = jnp.zeros_like(zero_ref)

        def fill(row0):
            copy = pltpu.make_async_copy(zero_ref, xs_ref.at[pl.ds(pl.multiple_of(row0, tile), tile)], sem.at[2])
            copy.start()
            copy.wait()

        for g in range(N_GROUPS):
            @pl.when(tail_ref[g] >= 0)
            def _():
                fill(tail_ref[g])

        def fill_unused(t, carry):
            fill(t * tile)
            return carry

        lax.fori_loop(used_ref[0], xs_ref.shape[0] // tile, fill_unused, 0)

    def stage_and_issue(slot):
        stage = cat_ref.at[slot]
        stage[:, 0:d] = h_ref[...]
        stage[:, d:] = gcol_ref[...]

        def issue(j, carry):
            for k in range(2):
                r = 2 * j + k
                _row_copy(stage, r, xs_ref, pos_ref[r], sem.at[slot]).start(priority=k)
            return carry

        lax.fori_loop(0, tm // 2, issue, 0)

    for slot in range(2):
        @pl.when(i % 2 == slot)
        def _():
            stage_and_issue(slot)

            @pl.when(i > 0)
            def _():
                _wait_rows(cat_ref.at[1 - slot], xs_ref, sem.at[1 - slot], tm)

            @pl.when(i == pl.num_programs(0) - 1)
            def _():
                _wait_rows(cat_ref.at[slot], xs_ref, sem.at[slot], tm)


def _dispatch(h, gcol, pos, tails, n_used, n_rows, tm, tile):
    n, d = h.shape
    width = d + LANES
    whole_smem = pl.BlockSpec(memory_space=pltpu.SMEM)
    return pl.pallas_call(
        functools.partial(_dispatch_kernel, tm=tm, tile=tile), grid=(n // tm,),
        in_specs=[pl.BlockSpec((tm,), lambda i: (i,), memory_space=pltpu.SMEM), whole_smem, whole_smem,
                  pl.BlockSpec((tm, d), lambda i: (i, 0)), pl.BlockSpec((tm, LANES), lambda i: (i, 0))],
        out_specs=pl.BlockSpec(memory_space=pl.ANY),
        out_shape=jax.ShapeDtypeStruct((n_rows, width), F32),
        scratch_shapes=[pltpu.VMEM((2, tm, width), F32), pltpu.VMEM((tile, width), F32),
                        pltpu.SemaphoreType.DMA((3,))],
        compiler_params=_params(("arbitrary",)), name="moe_dispatch")(pos, tails, n_used, h, gcol)


def _expert_kernel(src_ref, grp_ref, used_ref, xs_ref, w1_ref, w3_ref, w2_ref, ys_ref):
    t = pl.program_id(0)
    d = ys_ref.shape[1]

    @pl.when(t < used_ref[0])
    def _():
        x = xs_ref[:, 0:d].astype(BF16)
        gates = xs_ref[:, d:]
        acc = jnp.zeros(ys_ref.shape, F32)
        for e in range(EXPERTS_PER_GROUP):
            cols = slice(e * EXPERT_FF, (e + 1) * EXPERT_FF)
            a = _dot(x, w1_ref[0, :, cols])
            hid = a * (0.5 + 0.5 * jnp.tanh(0.5 * a)) * _dot(x, w3_ref[0, :, cols])
            acc = acc + _dot((hid * gates[:, e:e + 1]).astype(BF16), w2_ref[0, cols, :])
        ys_ref[...] = acc

    @pl.when(t >= used_ref[0])
    def _():
        ys_ref[...] = jnp.zeros_like(ys_ref)


def _experts(xs, tile_src, tile_grp, n_used, w1s, w3s, w2s, tile):
    n_rows, width = xs.shape
    d = w1s.shape[1]
    ff = w1s.shape[2]
    grid_spec = pltpu.PrefetchScalarGridSpec(
        num_scalar_prefetch=3, grid=(n_rows // tile,),
        in_specs=[pl.BlockSpec((tile, width), lambda t, src, grp, used: (src[t], 0)),
                  pl.BlockSpec((1, d, ff), lambda t, src, grp, used: (grp[t], 0, 0)),
                  pl.BlockSpec((1, d, ff), lambda t, src, grp, used: (grp[t], 0, 0)),
                  pl.BlockSpec((1, ff, d), lambda t, src, grp, used: (grp[t], 0, 0))],
        out_specs=pl.BlockSpec((tile, d), lambda t, src, grp, used: (t, 0)))
    return pl.pallas_call(
        _expert_kernel, grid_spec=grid_spec, out_shape=jax.ShapeDtypeStruct((n_rows, d), F32),
        compiler_params=_params(("arbitrary",)), name="moe_experts",
    )(tile_src, tile_grp, n_used, xs, w1s, w3s, w2s)


def _combine_kernel(pos_ref, nxt_ref, ys_ref, h_ref, g_ref, b_ref, ho_ref, hbo_ref, y_ref, sem, *, tm):
    i = pl.program_id(0)

    def fetch(idx_ref, slot):
        def issue(j, carry):
            for k in range(2):
                r = 2 * j + k
                _row_copy(ys_ref, idx_ref[r], y_ref.at[slot], r, sem.at[slot]).start(priority=k)
            return carry

        lax.fori_loop(0, tm // 2, issue, 0)

    @pl.when(i == 0)
    def _():
        fetch(pos_ref, 0)

    for slot in range(2):
        @pl.when(i % 2 == slot)
        def _():
            @pl.when(i + 1 < pl.num_programs(0))
            def _():
                fetch(nxt_ref, 1 - slot)

            _wait_rows(ys_ref, y_ref.at[slot], sem.at[slot], tm)
            out = _layer_norm(ALPHA * h_ref[...] + y_ref[slot], g_ref[...], b_ref[...])
            ho_ref[...] = out
            hbo_ref[...] = out.astype(BF16)


def _combine(ys, pos, h, g, b, tm):
    n, d = h.shape
    last = n // tm - 1
    row = pl.BlockSpec((tm, d), lambda i: (i, 0))
    vec = pl.BlockSpec((1, d), lambda i: (0, 0))
    return pl.pallas_call(
        functools.partial(_combine_kernel, tm=tm), grid=(n // tm,),
        in_specs=[pl.BlockSpec((tm,), lambda i: (i,), memory_space=pltpu.SMEM),
                  pl.BlockSpec((tm,), lambda i: (jnp.minimum(i + 1, last),), memory_space=pltpu.SMEM),
                  pl.BlockSpec(memory_space=pl.ANY), row, vec, vec],
        out_specs=[row, row],
        out_shape=[jax.ShapeDtypeStruct((n, d), F32), jax.ShapeDtypeStruct((n, d), BF16)],
        scratch_shapes=[pltpu.VMEM((2, tm, d), F32), pltpu.SemaphoreType.DMA((2,))],
        compiler_params=_params(("arbitrary",)), name="moe_combine")(pos, pos, ys, h, g, b)


def _routed_moe(h, rw_parts, rb_col, w1s, w3s, w2s, g, b, tile):
    n, d = h.shape
    gates, picks, counts = _router(h, rw_parts, rb_col, min(512, n))
    count = counts[:, 0].astype(jnp.int32)
    padded = (count + tile - 1) // tile * tile
    ends = jnp.cumsum(padded)
    starts = ends - padded
    n_tiles = n // tile + N_GROUPS
    n_used = (ends[-1] // tile).reshape(1).astype(jnp.int32)
    tile_src = jnp.minimum(jnp.arange(n_tiles, dtype=jnp.int32), n_used - 1)
    tile_grp = jnp.minimum(jnp.sum((tile_src[:, None] * tile >= ends[None, :N_GROUPS]).astype(jnp.int32), axis=1),
                           N_GROUPS - 1)
    tails = jnp.where(padded > 0, ends - tile, -1).astype(jnp.int32)

    pos, gcol = _moe_pos(picks, gates, starts.astype(F32).reshape(ROUTE_ROWS, 1), min(512, n))
    xs = _dispatch(h, gcol, pos[0], tails, n_used, n_tiles * tile, min(512, n), tile)
    ys = _experts(xs, tile_src, tile_grp, n_used, w1s, w3s, w2s, tile)
    return _combine(ys, pos[0], h, g, b, min(256, n))


def _pack_in_weights(w_in):
    sizes = (256, 256, 256, FOX_HEADS, 256, 256, 256, HGRN_W, HGRN_W, HGRN_W, HGRN_W)
    offs = np.cumsum((0,) + sizes)
    fq, fk, fv, ff, dq, dk, dv, hq, hf, hi, hg = (w_in[:, offs[i]:offs[i + 1]] for i in range(len(sizes)))
    fox_scale = FOX_DIM ** -0.5 * LOG2E
    diff_scale = DIFF_QK_DIM ** -0.5 * LOG2E
    main = jnp.concatenate([fq * fox_scale, fk, fv, dq * diff_scale, dk, dv, hq, hi, hg], axis=1).astype(BF16)
    ff_pad = jnp.zeros((w_in.shape[0], LANES), w_in.dtype).at[:, :FOX_HEADS].set(ff)
    gate = jnp.concatenate([hf, ff_pad], axis=1).astype(BF16)
    return main, gate


def kernel(x, mem, ln_in_g, ln_in_b, w_in, fox_fb, lam_q1, lam_k1, lam_q2, lam_k2, diff_norm_g, hgrn_lb,
           hgrn_norm_g, w_out, mem_wq, mem_wk, mem_wv, mem_wo, router_w, router_b, w1, w3, w2, ln_g, ln_b):
    batch, seq, d = x.shape
    n = batch * seq
    tm = min(512, seq)
    attn_tile = min(512, seq)
    vec = lambda v: v.reshape(1, -1).astype(F32)

    h, hb = _entry_ln(x.reshape(n, d), vec(ln_in_g), vec(ln_in_b), tm)
    memb = mem.reshape(-1, d).astype(BF16)
    rw_parts = jnp.stack(_split3(router_w.T.astype(F32)))
    rb_col = router_b.reshape(N_EXPERTS, 1).astype(F32)
    hgrn_lb = hgrn_lb.astype(F32)

    for i in range(DEPTH):
        lam_init = 0.8 - 0.6 * math.exp(-0.3 * i)
        w_main, w_gate = _pack_in_weights(w_in[i])
        p_main, g_gate = _in_proj(hb, w_main, w_gate, tm)

        q_aug, k_aug, fvt, dvt = _fox_prep(p_main, g_gate, fox_fb[i].astype(F32), batch, seq, min(512, seq))
        y_fox = _fox_attn(q_aug, k_aug, fvt, attn_tile)

        lam_pack = jnp.zeros((8, LANES), F32)
        for r, v in enumerate((lam_q1[i], lam_k1[i], lam_q2[i], lam_k2[i])):
            lam_pack = lam_pack.at[r, :DIFF_QK_DIM].set(v.astype(F32))
        diff_g = jnp.tile(vec(diff_norm_g[i]), (1, DIFF_HEADS))
        y_diff = _diff_attn(p_main, dvt, lam_pack, diff_g, lam_init, batch, seq, attn_tile)

        y_hgrn = _hgrn(p_main, g_gate, hgrn_lb, vec(hgrn_norm_g[i]), i, batch, seq, min(256, seq))
        h, hb = _proj_ln(y_fox, y_diff, y_hgrn, w_out[i].astype(BF16), h, vec(ln_g[i, 0]), vec(ln_b[i, 0]), tm)

        kmem = _matmul(memb, mem_wk[i].astype(BF16), BF16, memb.shape[0], 512, "mem_k")
        vmem = _matmul(memb, mem_wv[i].astype(BF16), BF16, memb.shape[0], 512, "mem_v")
        wq = (mem_wq[i] * (MEM_DIM ** -0.5 * LOG2E)).astype(BF16)
        h = _mem_attn(hb, h, wq, kmem, vmem, mem_wo[i].astype(BF16), vec(ln_g[i, 1]), vec(ln_b[i, 1]),
                          batch, seq, tm)

        stack_cols = lambda w: (w.reshape(N_GROUPS, EXPERTS_PER_GROUP, d, EXPERT_FF).transpose(0, 2, 1, 3)
                                .reshape(N_GROUPS, d, EXPERTS_PER_GROUP * EXPERT_FF).astype(BF16))
        w2s = w2[i].reshape(N_GROUPS, EXPERTS_PER_GROUP * EXPERT_FF, d).astype(BF16)
        h, hb = _routed_moe(h, rw_parts, rb_col, stack_cols(w1[i]), stack_cols(w3[i]), w2s,
                            vec(ln_g[i, 2]), vec(ln_b[i, 2]), min(512, n // 8))
    return h.reshape(batch, seq, d)
```

```python
import functools
import math

import numpy as np
import jax
import jax.numpy as jnp
from jax import lax
from jax.experimental import pallas as pl
from jax.experimental.pallas import tpu as pltpu

F32 = jnp.float32
BF16 = jnp.bfloat16

D_MODEL = 1024
DEPTH = 2
FOX_DIM = 64
FOX_HEADS = 4
DIFF_QK_DIM = 32
DIFF_V_DIM = 64
DIFF_HEADS = 4
HGRN_DIM = 128
HGRN_HEADS = 4
HGRN_W = HGRN_DIM * HGRN_HEADS
HGRN_CHUNK = 64
MEM_HEADS = 4
MEM_DIM = D_MODEL // MEM_HEADS
N_EXPERTS = 16
N_GROUPS = 4
EXPERTS_PER_GROUP = 4
EXPERT_FF = D_MODEL // 2
ALPHA = (2 * DEPTH) ** 0.25
LN_EPS = 1e-5
LOG2E = 1.4426950408889634
NEG_INIT = -1e30
LANES = 128
V_ROWS = 80
VMEM_LIMIT = 56 * 1024 * 1024

C_FQ, C_FK, C_FV, C_DQ, C_DK, C_DV, C_HQ, C_HI, C_HG = (0, 256, 512, 768, 1024, 1280, 1536, 2048, 2560)
MAIN_W = 3072
GATE_W = HGRN_W + LANES


def _params(sem, vmem=VMEM_LIMIT, flags=None):
    return pltpu.CompilerParams(dimension_semantics=sem, vmem_limit_bytes=vmem, flags=flags)


def _split3(x):
    hi = x.astype(BF16)
    r = x - hi.astype(F32)
    mid = r.astype(BF16)
    lo = (r - mid.astype(F32)).astype(BF16)
    return hi, mid, lo


def _dot(a, b):
    return jnp.dot(a, b, preferred_element_type=F32)


def _dot_nt(a, b):
    return lax.dot_general(a, b, (((1,), (1,)), ((), ())), preferred_element_type=F32)


def _sigmoid(x):
    return 1.0 / (1.0 + jnp.exp(-x))


def _log_sigmoid(x):
    return jnp.minimum(x, 0.0) - jnp.log(1.0 + jnp.exp(-jnp.abs(x)))


def _layer_norm(x, g, b):
    mu = jnp.mean(x, axis=-1, keepdims=True)
    xc = x - mu
    var = jnp.mean(xc * xc, axis=-1, keepdims=True)
    return xc * lax.rsqrt(var + LN_EPS) * g + b


def _ln_kernel(x_ref, g_ref, b_ref, h_ref, hb_ref):
    y = _layer_norm(x_ref[...], g_ref[...], b_ref[...])
    h_ref[...] = y
    hb_ref[...] = y.astype(BF16)


def _entry_ln(x, g, b, tm):
    n, d = x.shape
    row = pl.BlockSpec((tm, d), lambda i: (i, 0))
    vec = pl.BlockSpec((1, d), lambda i: (0, 0))
    return pl.pallas_call(
        _ln_kernel, grid=(n // tm,), in_specs=[row, vec, vec], out_specs=[row, row],
        out_shape=[jax.ShapeDtypeStruct((n, d), F32), jax.ShapeDtypeStruct((n, d), BF16)],
        compiler_params=_params(("parallel",)), name="entry_ln")(x, g, b)


def _mm_kernel(x_ref, w_ref, o_ref):
    o_ref[...] = _dot(x_ref[...], w_ref[...]).astype(o_ref.dtype)


def _matmul(x, w, out_dtype, tm, tn, name):
    n, k = x.shape
    m = w.shape[1]
    return pl.pallas_call(
        _mm_kernel, grid=(n // tm, m // tn),
        in_specs=[pl.BlockSpec((tm, k), lambda i, j: (i, 0)), pl.BlockSpec((k, tn), lambda i, j: (0, j))],
        out_specs=pl.BlockSpec((tm, tn), lambda i, j: (i, j)),
        out_shape=jax.ShapeDtypeStruct((n, m), out_dtype),
        compiler_params=_params(("parallel", "arbitrary")), name=name)(x, w)


IN_PROJ_CHUNK = 512


def _in_proj_kernel(x_ref, wm_ref, wg_ref, pm_ref, pg_ref):
    x = x_ref[...]
    for c0 in range(0, MAIN_W, IN_PROJ_CHUNK):
        cs = slice(c0, c0 + IN_PROJ_CHUNK)
        pm_ref[:, cs] = _dot(x, wm_ref[:, cs]).astype(BF16)
    pg_ref[...] = _dot(x, wg_ref[...])


def _in_proj(hb, w_main, w_gate, tm):
    n, d = hb.shape
    return pl.pallas_call(
        _in_proj_kernel, grid=(n // tm,),
        in_specs=[pl.BlockSpec((tm, d), lambda i: (i, 0)),
                  pl.BlockSpec((d, MAIN_W), lambda i: (0, 0)), pl.BlockSpec((d, GATE_W), lambda i: (0, 0))],
        out_specs=[pl.BlockSpec((tm, MAIN_W), lambda i: (i, 0)), pl.BlockSpec((tm, GATE_W), lambda i: (i, 0))],
        out_shape=[jax.ShapeDtypeStruct((n, MAIN_W), BF16), jax.ShapeDtypeStruct((n, GATE_W), F32)],
        compiler_params=_params(("parallel",)), name="in_proj")(hb, w_main, w_gate)


def _value_rows(v_ref, vt_ref, heads, dim):
    ts = v_ref.shape[0]
    vt = v_ref[...].astype(F32).T.astype(BF16)
    tail_row = lax.broadcasted_iota(jnp.int32, (V_ROWS - dim, ts), 0)
    tail = jnp.where(tail_row == 0, 1.0, 0.0).astype(BF16)
    for h in range(heads):
        vt_ref[0, h, 0:dim, :] = vt[h * dim:(h + 1) * dim]
        vt_ref[0, h, dim:V_ROWS, :] = tail


def _fox_prep_kernel(fq_ref, fk_ref, fv_ref, dv_ref, ff_ref, fb_ref, tri_ref, sel_ref, eq_ref, ek_ref, cq_ref,
                     ck_ref, qa_ref, ka_ref, fvt_ref, dvt_ref, carry_ref):
    @pl.when(pl.program_id(1) == 0)
    def _():
        carry_ref[...] = jnp.zeros_like(carry_ref)

    _value_rows(fv_ref, fvt_ref, FOX_HEADS, FOX_DIM)
    _value_rows(dv_ref, dvt_ref, DIFF_HEADS, DIFF_V_DIM)

    logf = _log_sigmoid(ff_ref[...] + fb_ref[...])
    tri = tri_ref[...]
    hi, mid, lo = _split3(logf)
    cum = _dot(tri, hi) + _dot(tri, mid) + _dot(tri, lo) + carry_ref[...]
    ts = cum.shape[0]
    carry_ref[...] = cum[ts - 1:ts, :]
    parts = jnp.concatenate(_split3(cum * LOG2E), axis=1)
    fq = fq_ref[...]
    fk = fk_ref[...]
    for h in range(FOX_HEADS):
        pair = slice((h // 2) * LANES, (h // 2 + 1) * LANES)
        sel = sel_ref[h % 2]
        qa = _dot(fq[:, pair], sel) + _dot(parts, eq_ref[h]) + cq_ref[...]
        ka = _dot(fk[:, pair], sel) + _dot(parts, ek_ref[h]) + ck_ref[...]
        qa_ref[0, h] = qa.astype(BF16)
        ka_ref[0, h] = ka.astype(BF16)


def _fox_prep_consts():
    sel = np.zeros((2, LANES, LANES), np.float32)
    for half in range(2):
        for d in range(FOX_DIM):
            sel[half, half * FOX_DIM + d, d] = 1.0
    eq = np.zeros((FOX_HEADS, 3 * LANES, LANES), np.float32)
    ek = np.zeros((FOX_HEADS, 3 * LANES, LANES), np.float32)
    for h in range(FOX_HEADS):
        for p in range(3):
            eq[h, p * LANES + h, FOX_DIM + p] = 1.0
            ek[h, p * LANES + h, FOX_DIM + 3 + p] = -1.0
    cq = np.zeros((1, LANES), np.float32)
    ck = np.zeros((1, LANES), np.float32)
    cq[0, FOX_DIM + 3:FOX_DIM + 6] = 1.0
    ck[0, FOX_DIM:FOX_DIM + 3] = 1.0
    return (jnp.asarray(sel, BF16), jnp.asarray(eq, BF16), jnp.asarray(ek, BF16), jnp.asarray(cq), jnp.asarray(ck))


def _fox_prep(p_main, g_gate, fb, batch, seq, ts):
    ns = seq // ts
    tri = jnp.asarray(np.tril(np.ones((ts, ts), np.float32)), BF16)
    sel, eq, ek, cq, ck = _fox_prep_consts()
    fb_pad = jnp.zeros((1, LANES), F32).at[0, :FOX_HEADS].set(fb)
    const2 = lambda shape: pl.BlockSpec(shape, lambda b, s: (0,) * len(shape))
    out_spec = pl.BlockSpec((1, FOX_HEADS, ts, LANES), lambda b, s: (b, 0, s, 0))
    out_sds = jax.ShapeDtypeStruct((batch, FOX_HEADS, seq, LANES), BF16)
    vt_spec = pl.BlockSpec((1, FOX_HEADS, V_ROWS, ts), lambda b, s: (b, 0, 0, s))
    vt_sds = jax.ShapeDtypeStruct((batch, FOX_HEADS, V_ROWS, seq), BF16)
    col = lambda c0: pl.BlockSpec((ts, 256), lambda b, s: (b * ns + s, c0 // 256))
    return pl.pallas_call(
        _fox_prep_kernel, grid=(batch, ns),
        in_specs=[col(C_FQ), col(C_FK), col(C_FV), col(C_DV),
                  pl.BlockSpec((ts, LANES), lambda b, s: (b * ns + s, HGRN_W // LANES)),
                  const2((1, LANES)), const2((ts, ts)), const2((2, LANES, LANES)),
                  const2((FOX_HEADS, 3 * LANES, LANES)), const2((FOX_HEADS, 3 * LANES, LANES)),
                  const2((1, LANES)), const2((1, LANES))],
        out_specs=[out_spec, out_spec, vt_spec, vt_spec], out_shape=[out_sds, out_sds, vt_sds, vt_sds],
        scratch_shapes=[pltpu.VMEM((1, LANES), F32)],
        compiler_params=_params(("arbitrary", "arbitrary")), name="fox_prep",
    )(p_main, p_main, p_main, p_main, g_gate, fb_pad, tri, sel, eq, ek, cq, ck)


def _flash_sweep(qi, n_streams, score_fn, value_fn, visible, sa_ref, sb_ref, m_ref, acc_ref):
    m_ref[...] = jnp.full(m_ref.shape, NEG_INIT, F32)
    acc_ref[...] = jnp.zeros(acc_ref.shape, F32)

    def consume(src_ref, st, j, masked):
        s = src_ref[st]
        if masked:
            s = jnp.where(visible, s, -jnp.inf)
        m = m_ref[st]
        m_new = jnp.maximum(m, jnp.max(s, axis=0, keepdims=True))
        p = jnp.exp2(s - m_new).astype(BF16)
        acc_ref[st] = jnp.exp2(m - m_new) * acc_ref[st] + _dot(value_fn(st, j), p)
        m_ref[st] = m_new

    def stage(dst_ref, src_ref, j_next, j_cur):
        for st in range(n_streams):
            dst_ref[st] = score_fn(st, j_next)
            consume(src_ref, st, j_cur, False)

    for st in range(n_streams):
        sa_ref[st] = score_fn(st, 0)

    def pair(p, carry):
        stage(sb_ref, sa_ref, 2 * p + 1, 2 * p)
        stage(sa_ref, sb_ref, 2 * p + 2, 2 * p + 1)
        return carry

    lax.fori_loop(0, qi // 2, pair, 0)

    @pl.when(qi % 2 == 0)
    def _():
        for st in range(n_streams):
            consume(sa_ref, st, qi, True)

    @pl.when(qi % 2 == 1)
    def _():
        stage(sb_ref, sa_ref, qi, qi - 1)
        for st in range(n_streams):
            consume(sb_ref, st, qi, True)


def _flash_scratch(n_streams, tile):
    return [pltpu.VMEM((n_streams, tile, tile), F32), pltpu.VMEM((n_streams, tile, tile), F32),
            pltpu.VMEM((n_streams, 1, tile), F32), pltpu.VMEM((n_streams, V_ROWS, tile), F32)]


def _finish(acc, dim):
    return acc[:dim] / acc[dim:dim + 1]


def _transposed(o, tile):
    dim = o.shape[0]
    padded = jnp.concatenate([o, jnp.zeros((LANES - dim, tile), F32)], axis=0)
    return padded.T[:, :dim]


def _key_block(j, tile):
    return pl.ds(pl.multiple_of(j * tile, tile), tile)


def _fox_attn_kernel(q_ref, k_ref, vt_ref, o_ref, qt_ref, sa_ref, sb_ref, m_ref, acc_ref, *, tile):
    qi = pl.program_id(1)
    for h in range(FOX_HEADS):
        qt_ref[h] = q_ref[0, h].astype(F32).T.astype(BF16)
    row = lax.broadcasted_iota(jnp.int32, (tile, tile), 0)
    col = lax.broadcasted_iota(jnp.int32, (tile, tile), 1)
    visible = row <= col
    _flash_sweep(qi, FOX_HEADS,
                 lambda h, j: _dot(k_ref[0, h, _key_block(j, tile), :], qt_ref[h]),
                 lambda h, j: vt_ref[0, h, :, _key_block(j, tile)],
                 visible, sa_ref, sb_ref, m_ref, acc_ref)
    o = jnp.concatenate([_finish(acc_ref[h], FOX_DIM) for h in range(FOX_HEADS)], axis=0)
    o_ref[...] = o.T.astype(o_ref.dtype)


def _fox_attn(q_aug, k_aug, vt_aug, tile):
    batch, heads, seq, _ = q_aug.shape
    return pl.pallas_call(
        functools.partial(_fox_attn_kernel, tile=tile), grid=(batch, seq // tile),
        in_specs=[pl.BlockSpec((1, heads, tile, LANES), lambda b, i: (b, 0, i, 0)),
                  pl.BlockSpec((1, heads, seq, LANES), lambda b, i: (b, 0, 0, 0)),
                  pl.BlockSpec((1, heads, V_ROWS, seq), lambda b, i: (b, 0, 0, 0))],
        out_specs=pl.BlockSpec((tile, heads * FOX_DIM), lambda b, i: (b * (seq // tile) + i, 0)),
        out_shape=jax.ShapeDtypeStruct((batch * seq, heads * FOX_DIM), BF16),
        scratch_shapes=[pltpu.VMEM((heads, LANES, tile), BF16)] + _flash_scratch(heads, tile),
        compiler_params=_params(("parallel", "arbitrary")), name="fox_attn")(q_aug, k_aug, vt_aug)


def _diff_attn_kernel(q_ref, k_ref, vt_ref, lam_ref, g_ref, o_ref, q12_ref, sa_ref, sb_ref, m_ref, acc_ref,
                      *, tile, lam_init):
    qi = pl.program_id(1)
    qt = q_ref[...].astype(F32).T
    feat = lax.broadcasted_iota(jnp.int32, qt.shape, 0)
    for st in range(2 * DIFF_HEADS):
        keep = (feat >= DIFF_QK_DIM * st) & (feat < DIFF_QK_DIM * (st + 1))
        q12_ref[st] = jnp.where(keep, qt, 0.0).astype(BF16)
    row = lax.broadcasted_iota(jnp.int32, (tile, tile), 0)
    col = lax.broadcasted_iota(jnp.int32, (tile, tile), 1)
    visible = (row // HGRN_CHUNK) <= (col // HGRN_CHUNK)
    _flash_sweep(qi, 2 * DIFF_HEADS,
                 lambda st, j: _dot(k_ref[_key_block(j, tile), :], q12_ref[st]),
                 lambda st, j: vt_ref[0, st // 2, :, _key_block(j, tile)],
                 visible, sa_ref, sb_ref, m_ref, acc_ref)

    lp = lam_ref[...]
    lam = (jnp.exp(jnp.sum(lp[0:1] * lp[1:2], axis=1, keepdims=True))
           - jnp.exp(jnp.sum(lp[2:3] * lp[3:4], axis=1, keepdims=True)) + lam_init)
    outs = []
    for h in range(DIFF_HEADS):
        o = _finish(acc_ref[2 * h], DIFF_V_DIM) - lam * _finish(acc_ref[2 * h + 1], DIFF_V_DIM)
        outs.append(o * lax.rsqrt(jnp.mean(o * o, axis=0, keepdims=True) + LN_EPS))
    y = jnp.concatenate(outs, axis=0).T * g_ref[...] * (1.0 - lam_init)
    o_ref[...] = y.astype(o_ref.dtype)


def _diff_attn(p_main, dvt_aug, lam_pack, norm_g, lam_init, batch, seq, tile):
    heads = DIFF_HEADS
    width = heads * DIFF_V_DIM
    nq = seq // tile
    return pl.pallas_call(
        functools.partial(_diff_attn_kernel, tile=tile, lam_init=lam_init), grid=(batch, nq),
        in_specs=[pl.BlockSpec((tile, width), lambda b, i: (b * nq + i, C_DQ // width)),
                  pl.BlockSpec((seq, width), lambda b, i: (b, C_DK // width)),
                  pl.BlockSpec((1, heads, V_ROWS, seq), lambda b, i: (b, 0, 0, 0)),
                  pl.BlockSpec((8, LANES), lambda b, i: (0, 0)),
                  pl.BlockSpec((1, width), lambda b, i: (0, 0))],
        out_specs=pl.BlockSpec((tile, width), lambda b, i: (b * nq + i, 0)),
        out_shape=jax.ShapeDtypeStruct((batch * seq, width), BF16),
        scratch_shapes=[pltpu.VMEM((2 * heads, width, tile), BF16)] + _flash_scratch(2 * heads, tile),
        compiler_params=_params(("parallel", "arbitrary")), name="diff_attn",
    )(p_main, p_main, dvt_aug, lam_pack, norm_g)


HGRN_SAFE_SPAN = 80.0


def _hgrn_kernel(hq_ref, hi_ref, hg_ref, hf_ref, lb_ref, g_ref, tri_ref, o_ref, state_ref, score_ref,
                 q_ref, k_ref, bc_ref, *, layer, tc):
    @pl.when(pl.program_id(1) == 0)
    def _():
        state_ref[...] = jnp.zeros_like(state_ref)

    lb_all = lb_ref[...]
    e = jnp.exp(lb_all - jnp.max(lb_all, axis=0, keepdims=True))
    prob = e / jnp.sum(e, axis=0, keepdims=True)
    lb_row = jnp.zeros((1, HGRN_W), F32)
    for j in range(1, layer + 1):
        lb_row = lb_row + prob[j:j + 1]
    lb_row = jnp.maximum(lb_row, 0.0)

    c = HGRN_CHUNK
    n_chunks = tc // c
    half = c // 2 - 1
    r_i = lax.broadcasted_iota(jnp.int32, (c, c), 0)
    c_i = lax.broadcasted_iota(jnp.int32, (c, c), 1)
    causal = r_i >= c_i

    log_lb = jnp.log(lb_row)
    log_1m = jnp.log(1.0 - lb_row)
    z = hf_ref[...]
    ez = jnp.exp(-jnp.abs(z))
    b_term = log_1m + jnp.minimum(z, 0.0) - jnp.log(1.0 + ez)
    logf = jnp.maximum(log_lb, b_term) + jnp.log(1.0 + jnp.exp(-jnp.abs(log_lb - b_term)))
    kk = (1.0 - lb_row) * jnp.where(z >= 0.0, ez, 1.0) / (1.0 + ez)
    xq = hq_ref[...].astype(F32)
    q = xq * (0.5 + 0.5 * jnp.tanh(0.5 * xq))
    xg = hg_ref[...].astype(F32)
    gate = xg * (0.5 + 0.5 * jnp.tanh(0.5 * xg))
    v = hi_ref[...]
    vt = v.astype(F32).T.astype(BF16)

    tri = tri_ref[...]
    g_hi, g_mid, g_lo = _split3(logf)
    bc = _dot(tri, g_hi) + _dot(tri, g_mid) + _dot(tri, g_lo)

    rows = lambda r: jnp.concatenate(
        [jnp.broadcast_to(bc[ci * c + r:ci * c + r + 1], (c, HGRN_W)) for ci in range(n_chunks)], axis=0)
    ref_full = rows(half)
    last_full = rows(c - 1)
    span = bc - ref_full
    q_mid = (q * jnp.exp(span)).astype(BF16)
    k_mid = (kk * jnp.exp(-span)).astype(BF16)
    q_in = (q * jnp.exp(bc)).astype(BF16)
    k_out = (kk * jnp.exp(last_full - bc)).astype(BF16)

    units = [(ci, h) for ci in range(n_chunks) for h in range(HGRN_HEADS)]
    sl = lambda ci, h: (slice(ci * c, (ci + 1) * c), slice(h * HGRN_DIM, (h + 1) * HGRN_DIM))
    for idx, u in enumerate(units):
        score_ref[idx] = jnp.where(causal, _dot_nt(q_mid[sl(*u)], k_mid[sl(*u)]), 0.0)

    @pl.when(jnp.max(jnp.abs(span)) > HGRN_SAFE_SPAN)
    def _():
        q_ref[...] = q
        k_ref[...] = kk
        bc_ref[...] = bc
        for idx, (ci, h) in enumerate(units):
            rs, hs = sl(ci, h)
            q_c = q_ref[rs, hs]
            bc_c = bc_ref[rs, hs]

            def columns(a, acc):
                base = pl.multiple_of(ci * c + a * 8, 8)
                k_8 = k_ref[pl.ds(base, 8), hs]
                b_8 = bc_ref[pl.ds(base, 8), hs]
                for j in range(8):
                    decay = jnp.exp(jnp.minimum(bc_c - b_8[j:j + 1], 0.0))
                    col = jnp.sum(q_c * (k_8[j:j + 1] * decay), axis=1, keepdims=True)
                    acc = jnp.where(c_i == a * 8 + j, col, acc)
                return acc

            exact = lax.fori_loop(0, c // 8, columns, jnp.zeros((c, c), F32))
            score_ref[idx] = jnp.where(causal, exact, 0.0)

    intra = {u: _dot(score_ref[idx].astype(BF16), v[sl(*u)]) for idx, u in enumerate(units)}
    updates = {(ci, h): _dot(vt[h * HGRN_DIM:(h + 1) * HGRN_DIM, ci * c:(ci + 1) * c], k_out[sl(ci, h)])
               for ci, h in units}

    states = [state_ref[h] for h in range(HGRN_HEADS)]
    for ci in range(n_chunks):
        for h in range(HGRN_HEADS):
            rs, hs = sl(ci, h)
            o = intra[(ci, h)] + _dot_nt(q_in[rs, hs], states[h].astype(BF16))
            decay = jnp.exp(bc[ci * c + c - 1:ci * c + c, hs])
            states[h] = states[h] * decay + updates[(ci, h)]
            y = o * lax.rsqrt(jnp.mean(o * o, axis=1, keepdims=True) + LN_EPS) * g_ref[...] * gate[rs, hs]
            o_ref[rs, hs] = y.astype(o_ref.dtype)
    for h in range(HGRN_HEADS):
        state_ref[h] = states[h]


def _hgrn(p_main, g_gate, hgrn_lb, norm_g, layer, batch, seq, tc):
    ns = seq // tc
    chunk_tri = np.tril(np.ones((HGRN_CHUNK, HGRN_CHUNK), np.float32))
    tri = jnp.asarray(np.kron(np.eye(tc // HGRN_CHUNK, dtype=np.float32), chunk_tri), BF16)
    col = lambda c0: pl.BlockSpec((tc, HGRN_W), lambda b, s: (b * ns + s, c0 // HGRN_W))
    return pl.pallas_call(
        functools.partial(_hgrn_kernel, layer=layer, tc=tc), grid=(batch, ns),
        in_specs=[col(C_HQ), col(C_HI), col(C_HG),
                  pl.BlockSpec((tc, HGRN_W), lambda b, s: (b * ns + s, 0)),
                  pl.BlockSpec((DEPTH, HGRN_W), lambda b, s: (0, 0)),
                  pl.BlockSpec((1, HGRN_DIM), lambda b, s: (0, 0)),
                  pl.BlockSpec((tc, tc), lambda b, s: (0, 0))],
        out_specs=pl.BlockSpec((tc, HGRN_W), lambda b, s: (b * ns + s, 0)),
        out_shape=jax.ShapeDtypeStruct((batch * seq, HGRN_W), BF16),
        scratch_shapes=[pltpu.VMEM((HGRN_HEADS, HGRN_DIM, HGRN_DIM), F32),
                        pltpu.VMEM((tc // HGRN_CHUNK * HGRN_HEADS, HGRN_CHUNK, HGRN_CHUNK), F32),
                        pltpu.VMEM((tc, HGRN_W), F32), pltpu.VMEM((tc, HGRN_W), F32), pltpu.VMEM((tc, HGRN_W), F32)],
        compiler_params=_params(("parallel", "arbitrary")), name="hgrn2",
    )(p_main, p_main, p_main, g_gate, hgrn_lb, norm_g, tri)


def _proj_ln_kernel(yf_ref, yd_ref, yh_ref, w_ref, h_ref, g_ref, b_ref, ho_ref, hbo_ref):
    fw = yf_ref.shape[1]
    dw = fw + yd_ref.shape[1]
    mix = _dot(yf_ref[...], w_ref[0:fw]) + _dot(yd_ref[...], w_ref[fw:dw]) + _dot(yh_ref[...], w_ref[dw:])
    out = _layer_norm(ALPHA * h_ref[...] + mix, g_ref[...], b_ref[...])
    ho_ref[...] = out
    hbo_ref[...] = out.astype(BF16)


def _proj_ln(y_fox, y_diff, y_hgrn, w, h, g, b, tm):
    n, d = h.shape
    row = lambda width: pl.BlockSpec((tm, width), lambda i: (i, 0))
    vec = pl.BlockSpec((1, d), lambda i: (0, 0))
    return pl.pallas_call(
        _proj_ln_kernel, grid=(n // tm,),
        in_specs=[row(y_fox.shape[1]), row(y_diff.shape[1]), row(y_hgrn.shape[1]),
                  pl.BlockSpec(w.shape, lambda i: (0, 0)), row(d), vec, vec],
        out_specs=[row(d), row(d)],
        out_shape=[jax.ShapeDtypeStruct((n, d), F32), jax.ShapeDtypeStruct((n, d), BF16)],
        compiler_params=_params(("parallel",)), name="out_proj_ln")(y_fox, y_diff, y_hgrn, w, h, g, b)


def _mem_attn_kernel(hb_ref, h_ref, wq_ref, k_ref, v_ref, wo_ref, g_ref, b_ref, ho_ref):
    q = _dot(hb_ref[...], wq_ref[...])
    outs = []
    for hd in range(MEM_HEADS):
        hs = slice(hd * MEM_DIM, (hd + 1) * MEM_DIM)
        s = _dot_nt(q[:, hs].astype(BF16), k_ref[:, hs])
        p = jnp.exp2(s - jnp.max(s, axis=1, keepdims=True))
        o = _dot(p.astype(BF16), v_ref[:, hs]) / jnp.sum(p, axis=1, keepdims=True)
        outs.append(o.astype(BF16))
    att = jnp.concatenate(outs, axis=1)
    z = ALPHA * h_ref[...] + _dot(att, wo_ref[...])
    ho_ref[...] = _layer_norm(z, g_ref[...], b_ref[...])


def _mem_attn(hb, h, wq, kmem, vmem, wo, g, b, batch, seq, tm):
    n, d = h.shape
    ns = seq // tm
    mem_len = kmem.shape[0] // batch
    row = pl.BlockSpec((tm, d), lambda bi, s: (bi * ns + s, 0))
    full = pl.BlockSpec((d, d), lambda bi, s: (0, 0))
    mem = pl.BlockSpec((mem_len, d), lambda bi, s: (bi, 0))
    vec = pl.BlockSpec((1, d), lambda bi, s: (0, 0))
    return pl.pallas_call(
        _mem_attn_kernel, grid=(batch, ns),
        in_specs=[row, row, full, mem, mem, full, vec, vec], out_specs=row,
        out_shape=jax.ShapeDtypeStruct((n, d), F32),
        compiler_params=_params(("parallel", "arbitrary")), name="mem_attn")(hb, h, wq, kmem, vmem, wo, g, b)


def _top2_sum(a, b, c, d):
    hi1, lo1 = jnp.maximum(a, b), jnp.minimum(a, b)
    hi2, lo2 = jnp.maximum(c, d), jnp.minimum(c, d)
    return jnp.maximum(hi1, hi2) + jnp.maximum(jnp.minimum(hi1, hi2), jnp.maximum(lo1, lo2))


def _router_kernel(h_ref, rw_ref, rb_ref, gate_ref, pick_ref, count_ref):
    xh, xm, xl = _split3(h_ref[...])
    rw = rw_ref[...]
    ph, pm, pl_ = _dot_nt(rw, xh), _dot_nt(rw, xm), _dot_nt(rw, xl)
    ne = N_EXPERTS
    logits = (ph[0:ne] + ph[ne:2 * ne] + pm[0:ne]) + (ph[2 * ne:] + pm[ne:2 * ne] + pl_[0:ne])
    scores = _sigmoid(logits)
    biased = scores + rb_ref[...]
    s_rows = [scores[e:e + 1] for e in range(N_EXPERTS)]
    b_rows = [biased[e:e + 1] for e in range(N_EXPERTS)]
    group_score = [_top2_sum(*b_rows[4 * g:4 * g + 4]) for g in range(N_GROUPS)]
    best = group_score[0]
    best_idx = jnp.zeros_like(best, dtype=jnp.int32)
    for g in range(1, N_GROUPS):
        better = group_score[g] > best
        best_idx = jnp.where(better, g, best_idx)
        best = jnp.where(better, group_score[g], best)
    picked = []
    for e in range(N_EXPERTS):
        g = e // EXPERTS_PER_GROUP
        rank = jnp.zeros_like(best)
        for j in range(4 * g, 4 * g + 4):
            if j == e:
                continue
            ahead = (b_rows[j] > b_rows[e]) | (b_rows[j] == b_rows[e]) if j < e else (b_rows[j] > b_rows[e])
            rank = rank + jnp.where(ahead, 1.0, 0.0)
        picked.append((best_idx == g) & (rank < 2.0))
    denom = jnp.zeros_like(best)
    for e in range(N_EXPERTS):
        denom = denom + jnp.where(picked[e], s_rows[e], 0.0)
    gate_ref[...] = jnp.zeros_like(gate_ref)
    pick_ref[...] = jnp.zeros_like(pick_ref)
    for slot in range(EXPERTS_PER_GROUP):
        gate = jnp.zeros_like(best)
        for g in range(N_GROUPS):
            e = g * EXPERTS_PER_GROUP + slot
            gate = gate + jnp.where(picked[e], s_rows[e] / denom, 0.0)
        gate_ref[slot:slot + 1, :] = gate
    for g in range(N_GROUPS):
        pick_ref[g:g + 1, :] = jnp.where(best_idx == g, 1.0, 0.0)

    @pl.when(pl.program_id(0) == 0)
    def _():
        count_ref[...] = jnp.zeros_like(count_ref)

    count_ref[...] += jnp.broadcast_to(jnp.sum(pick_ref[...], axis=1, keepdims=True), count_ref.shape)


ROUTE_ROWS = 8


def _router(h, rw_parts, rb_col, tm):
    n, d = h.shape
    rows = pl.BlockSpec((ROUTE_ROWS, tm), lambda i: (0, i))
    return pl.pallas_call(
        _router_kernel, grid=(n // tm,),
        in_specs=[pl.BlockSpec((tm, d), lambda i: (i, 0)),
                  pl.BlockSpec((3 * N_EXPERTS, d), lambda i: (0, 0)),
                  pl.BlockSpec((N_EXPERTS, 1), lambda i: (0, 0))],
        out_specs=[rows, rows, pl.BlockSpec((ROUTE_ROWS, LANES), lambda i: (0, 0))],
        out_shape=[jax.ShapeDtypeStruct((ROUTE_ROWS, n), F32), jax.ShapeDtypeStruct((ROUTE_ROWS, n), F32),
                   jax.ShapeDtypeStruct((ROUTE_ROWS, LANES), F32)],
        compiler_params=_params(("arbitrary",)), name="router")(h, rw_parts, rb_col)


def _moe_pos_kernel(pick_ref, gate_ref, start_ref, upper_ref, pos_ref, gcol_ref, carry_ref, rows_ref):
    @pl.when(pl.program_id(0) == 0)
    def _():
        carry_ref[...] = jnp.zeros_like(carry_ref)

    pick = pick_ref[...]
    rank = _dot(pick.astype(BF16), upper_ref[...]) + carry_ref[:, 0:1]
    slot = start_ref[...] + rank
    pos_ref[...] = jnp.sum(pick * slot, axis=0, keepdims=True).astype(jnp.int32)
    rows_ref[...] = jnp.zeros_like(rows_ref)
    rows_ref[0:ROUTE_ROWS, :] = gate_ref[...]
    gcol_ref[...] = rows_ref[...].T
    carry_ref[...] += jnp.broadcast_to(jnp.sum(pick, axis=1, keepdims=True), carry_ref.shape)


def _moe_pos(picks, gates, starts_col, tm):
    n = picks.shape[1]
    upper = jnp.asarray(np.triu(np.ones((tm, tm), np.float32), 1), BF16)
    rows = pl.BlockSpec((ROUTE_ROWS, tm), lambda i: (0, i))
    return pl.pallas_call(
        _moe_pos_kernel, grid=(n // tm,),
        in_specs=[rows, rows, pl.BlockSpec((ROUTE_ROWS, 1), lambda i: (0, 0)),
                  pl.BlockSpec((tm, tm), lambda i: (0, 0))],
        out_specs=[pl.BlockSpec((1, tm), lambda i: (0, i)), pl.BlockSpec((tm, LANES), lambda i: (i, 0))],
        out_shape=[jax.ShapeDtypeStruct((1, n), jnp.int32), jax.ShapeDtypeStruct((n, LANES), F32)],
        scratch_shapes=[pltpu.VMEM((ROUTE_ROWS, LANES), F32), pltpu.VMEM((LANES, tm), F32)],
        compiler_params=_params(("arbitrary",)), name="moe_pos")(picks, gates, starts_col, upper)


def _row_copy(src_ref, src_row, dst_ref, dst_row, sem):
    return pltpu.make_async_copy(src_ref.at[pl.ds(src_row, 1)], dst_ref.at[pl.ds(dst_row, 1)], sem)


DMA_WAIT_UNROLL = 8


def _wait_rows(src_ref, dst_ref, sem, n_rows):
    def drain(i, carry):
        _row_copy(src_ref, 0, dst_ref, 0, sem).wait()
        return carry

    lax.fori_loop(0, n_rows, drain, 0, unroll=DMA_WAIT_UNROLL)


def _dispatch_kernel(pos_ref, tail_ref, used_ref, h_ref, gcol_ref, xs_ref, cat_ref, zero_ref, sem, *, tm, tile):
    i = pl.program_id(0)
    d = h_ref.shape[1]

    @pl.when(i == 0)
    def _():
        zero_ref[...] = jnp.zeros_like(zero_ref)

        def fill(row0):
            copy = pltpu.make_async_copy(zero_ref, xs_ref.at[pl.ds(pl.multiple_of(row0, tile), tile)], sem.at[2])
            copy.start()
            copy.wait()

        for g in range(N_GROUPS):
            @pl.when(tail_ref[g] >= 0)
            def _():
                fill(tail_ref[g])

        def fill_unused(t, carry):
            fill(t * tile)
            return carry

        lax.fori_loop(used_ref[0], xs_ref.shape[0] // tile, fill_unused, 0)

    def stage_and_issue(slot):
        stage = cat_ref.at[slot]
        stage[:, 0:d] = h_ref[...]
        stage[:, d:] = gcol_ref[...]

        def issue(j, carry):
            for k in range(2):
                r = 2 * j + k
                _row_copy(stage, r, xs_ref, pos_ref[r], sem.at[slot]).start(priority=k)
            return carry

        lax.fori_loop(0, tm // 2, issue, 0)

    for slot in range(2):
        @pl.when(i % 2 == slot)
        def _():
            stage_and_issue(slot)

            @pl.when(i > 0)
            def _():
                _wait_rows(cat_ref.at[1 - slot], xs_ref, sem.at[1 - slot], tm)

            @pl.when(i == pl.num_programs(0) - 1)
            def _():
                _wait_rows(cat_ref.at[slot], xs_ref, sem.at[slot], tm)


def _dispatch(h, gcol, pos, tails, n_used, n_rows, tm, tile):
    n, d = h.shape
    width = d + LANES
    whole_smem = pl.BlockSpec(memory_space=pltpu.SMEM)
    return pl.pallas_call(
        functools.partial(_dispatch_kernel, tm=tm, tile=tile), grid=(n // tm,),
        in_specs=[pl.BlockSpec((tm,), lambda i: (i,), memory_space=pltpu.SMEM), whole_smem, whole_smem,
                  pl.BlockSpec((tm, d), lambda i: (i, 0)), pl.BlockSpec((tm, LANES), lambda i: (i, 0))],
        out_specs=pl.BlockSpec(memory_space=pl.ANY),
        out_shape=jax.ShapeDtypeStruct((n_rows, width), F32),
        scratch_shapes=[pltpu.VMEM((2, tm, width), F32), pltpu.VMEM((tile, width), F32),
                        pltpu.SemaphoreType.DMA((3,))],
        compiler_params=_params(("arbitrary",)), name="moe_dispatch")(pos, tails, n_used, h, gcol)


def _expert_kernel(src_ref, grp_ref, used_ref, xs_ref, w1_ref, w3_ref, w2_ref, ys_ref):
    t = pl.program_id(0)
    d = ys_ref.shape[1]

    @pl.when(t < used_ref[0])
    def _():
        x = xs_ref[:, 0:d].astype(BF16)
        gates = xs_ref[:, d:]
        acc = jnp.zeros(ys_ref.shape, F32)
        for e in range(EXPERTS_PER_GROUP):
            a = _dot(x, w1_ref[e])
            hid = a * (0.5 + 0.5 * jnp.tanh(0.5 * a)) * _dot(x, w3_ref[e])
            acc = acc + _dot((hid * gates[:, e:e + 1]).astype(BF16), w2_ref[e])
        ys_ref[...] = acc

    @pl.when(t >= used_ref[0])
    def _():
        ys_ref[...] = jnp.zeros_like(ys_ref)


def _experts(xs, tile_src, tile_grp, n_used, w1, w3, w2, tile):
    n_rows, width = xs.shape
    _, d, ff = w1.shape
    per = EXPERTS_PER_GROUP
    grid_spec = pltpu.PrefetchScalarGridSpec(
        num_scalar_prefetch=3, grid=(n_rows // tile,),
        in_specs=[pl.BlockSpec((tile, width), lambda t, src, grp, used: (src[t], 0)),
                  pl.BlockSpec((per, d, ff), lambda t, src, grp, used: (grp[t], 0, 0)),
                  pl.BlockSpec((per, d, ff), lambda t, src, grp, used: (grp[t], 0, 0)),
                  pl.BlockSpec((per, ff, d), lambda t, src, grp, used: (grp[t], 0, 0))],
        out_specs=pl.BlockSpec((tile, d), lambda t, src, grp, used: (t, 0)))
    return pl.pallas_call(
        _expert_kernel, grid_spec=grid_spec, out_shape=jax.ShapeDtypeStruct((n_rows, d), F32),
        compiler_params=_params(("arbitrary",)), name="moe_experts",
    )(tile_src, tile_grp, n_used, xs, w1, w3, w2)


def _combine_kernel(pos_ref, nxt_ref, ys_ref, h_ref, g_ref, b_ref, ho_ref, hbo_ref, y_ref, sem, *, tm):
    i = pl.program_id(0)

    def fetch(idx_ref, slot):
        def issue(j, carry):
            for k in range(2):
                r = 2 * j + k
                _row_copy(ys_ref, idx_ref[r], y_ref.at[slot], r, sem.at[slot]).start(priority=k)
            return carry

        lax.fori_loop(0, tm // 2, issue, 0)

    @pl.when(i == 0)
    def _():
        fetch(pos_ref, 0)

    for slot in range(2):
        @pl.when(i % 2 == slot)
        def _():
            @pl.when(i + 1 < pl.num_programs(0))
            def _():
                fetch(nxt_ref, 1 - slot)

            _wait_rows(ys_ref, y_ref.at[slot], sem.at[slot], tm)
            out = _layer_norm(ALPHA * h_ref[...] + y_ref[slot], g_ref[...], b_ref[...])
            ho_ref[...] = out
            hbo_ref[...] = out.astype(BF16)


def _combine(ys, pos, h, g, b, tm):
    n, d = h.shape
    last = n // tm - 1
    row = pl.BlockSpec((tm, d), lambda i: (i, 0))
    vec = pl.BlockSpec((1, d), lambda i: (0, 0))
    return pl.pallas_call(
        functools.partial(_combine_kernel, tm=tm), grid=(n // tm,),
        in_specs=[pl.BlockSpec((tm,), lambda i: (i,), memory_space=pltpu.SMEM),
                  pl.BlockSpec((tm,), lambda i: (jnp.minimum(i + 1, last),), memory_space=pltpu.SMEM),
                  pl.BlockSpec(memory_space=pl.ANY), row, vec, vec],
        out_specs=[row, row],
        out_shape=[jax.ShapeDtypeStruct((n, d), F32), jax.ShapeDtypeStruct((n, d), BF16)],
        scratch_shapes=[pltpu.VMEM((2, tm, d), F32), pltpu.SemaphoreType.DMA((2,))],
        compiler_params=_params(("arbitrary",)), name="moe_combine")(pos, pos, ys, h, g, b)


def _routed_moe(h, rw_parts, rb_col, w1s, w3s, w2s, g, b, tile):
    n, d = h.shape
    gates, picks, counts = _router(h, rw_parts, rb_col, min(2048, n))
    count = counts[:, 0].astype(jnp.int32)
    padded = (count + tile - 1) // tile * tile
    ends = jnp.cumsum(padded)
    starts = ends - padded
    n_tiles = n // tile + N_GROUPS
    n_used = (ends[-1] // tile).reshape(1).astype(jnp.int32)
    tile_src = jnp.minimum(jnp.arange(n_tiles, dtype=jnp.int32), n_used - 1)
    tile_grp = jnp.minimum(jnp.sum((tile_src[:, None] * tile >= ends[None, :N_GROUPS]).astype(jnp.int32), axis=1),
                           N_GROUPS - 1)
    tails = jnp.where(padded > 0, ends - tile, -1).astype(jnp.int32)

    pos, gcol = _moe_pos(picks, gates, starts.astype(F32).reshape(ROUTE_ROWS, 1), min(512, n))
    xs = _dispatch(h, gcol, pos[0], tails, n_used, n_tiles * tile, min(512, n), tile)
    ys = _experts(xs, tile_src, tile_grp, n_used, w1s, w3s, w2s, tile)
    return _combine(ys, pos[0], h, g, b, min(256, n))


def _pack_in_weights(w_in):
    sizes = (256, 256, 256, FOX_HEADS, 256, 256, 256, HGRN_W, HGRN_W, HGRN_W, HGRN_W)
    offs = np.cumsum((0,) + sizes)
    fq, fk, fv, ff, dq, dk, dv, hq, hf, hi, hg = (w_in[:, offs[i]:offs[i + 1]] for i in range(len(sizes)))
    fox_scale = FOX_DIM ** -0.5 * LOG2E
    diff_scale = DIFF_QK_DIM ** -0.5 * LOG2E
    main = jnp.concatenate([fq * fox_scale, fk, fv, dq * diff_scale, dk, dv, hq, hi, hg], axis=1).astype(BF16)
    ff_pad = jnp.zeros((w_in.shape[0], LANES), w_in.dtype).at[:, :FOX_HEADS].set(ff)
    gate = jnp.concatenate([hf, ff_pad], axis=1).astype(BF16)
    return main, gate


def kernel(x, mem, ln_in_g, ln_in_b, w_in, fox_fb, lam_q1, lam_k1, lam_q2, lam_k2, diff_norm_g, hgrn_lb,
           hgrn_norm_g, w_out, mem_wq, mem_wk, mem_wv, mem_wo, router_w, router_b, w1, w3, w2, ln_g, ln_b):
    batch, seq, d = x.shape
    n = batch * seq
    tm = min(512, seq)
    attn_tile = min(512, seq)
    vec = lambda v: v.reshape(1, -1).astype(F32)

    h, hb = _entry_ln(x.reshape(n, d), vec(ln_in_g), vec(ln_in_b), tm)
    memb = mem.reshape(-1, d).astype(BF16)
    rw_parts = jnp.concatenate(_split3(router_w.T.astype(F32)), axis=0)
    rb_col = router_b.reshape(N_EXPERTS, 1).astype(F32)
    hgrn_lb = hgrn_lb.astype(F32)

    for i in range(DEPTH):
        lam_init = 0.8 - 0.6 * math.exp(-0.3 * i)
        w_main, w_gate = _pack_in_weights(w_in[i])
        p_main, g_gate = _in_proj(hb, w_main, w_gate, tm)

        q_aug, k_aug, fvt, dvt = _fox_prep(p_main, g_gate, fox_fb[i].astype(F32), batch, seq, min(512, seq))
        y_fox = _fox_attn(q_aug, k_aug, fvt, attn_tile)

        lam_pack = jnp.zeros((8, LANES), F32)
        for r, v in enumerate((lam_q1[i], lam_k1[i], lam_q2[i], lam_k2[i])):
            lam_pack = lam_pack.at[r, :DIFF_QK_DIM].set(v.astype(F32))
        diff_g = jnp.tile(vec(diff_norm_g[i]), (1, DIFF_HEADS))
        y_diff = _diff_attn(p_main, dvt, lam_pack, diff_g, lam_init, batch, seq, attn_tile)

        y_hgrn = _hgrn(p_main, g_gate, hgrn_lb, vec(hgrn_norm_g[i]), i, batch, seq, min(256, seq))
        h, hb = _proj_ln(y_fox, y_diff, y_hgrn, w_out[i].astype(BF16), h, vec(ln_g[i, 0]), vec(ln_b[i, 0]), tm)

        kmem = _matmul(memb, mem_wk[i].astype(BF16), BF16, memb.shape[0], 512, "mem_k")
        vmem = _matmul(memb, mem_wv[i].astype(BF16), BF16, memb.shape[0], 512, "mem_v")
        wq = (mem_wq[i] * (MEM_DIM ** -0.5 * LOG2E)).astype(BF16)
        h = _mem_attn(hb, h, wq, kmem, vmem, mem_wo[i].astype(BF16), vec(ln_g[i, 1]), vec(ln_b[i, 1]),
                          batch, seq, tm)

        h, hb = _routed_moe(h, rw_parts, rb_col, w1[i].astype(BF16), w3[i].astype(BF16), w2[i].astype(BF16),
                            vec(ln_g[i, 2]), vec(ln_b[i, 2]), min(512, n // 8))
    return h.reshape(batch, seq, d)
```

```python
import functools
import math

import numpy as np
import jax
import jax.numpy as jnp
from jax import lax
from jax.experimental import pallas as pl
from jax.experimental.pallas import tpu as pltpu

F32 = jnp.float32
BF16 = jnp.bfloat16

D_MODEL = 1024
DEPTH = 2
FOX_DIM = 64
FOX_HEADS = 4
DIFF_QK_DIM = 32
DIFF_V_DIM = 64
DIFF_HEADS = 4
HGRN_DIM = 128
HGRN_HEADS = 4
HGRN_W = HGRN_DIM * HGRN_HEADS
HGRN_CHUNK = 64
MEM_HEADS = 4
MEM_DIM = D_MODEL // MEM_HEADS
N_EXPERTS = 16
N_GROUPS = 4
EXPERTS_PER_GROUP = 4
EXPERT_FF = D_MODEL // 2
ALPHA = (2 * DEPTH) ** 0.25
LN_EPS = 1e-5
LOG2E = 1.4426950408889634
NEG_INIT = -1e30
LANES = 128
V_ROWS = 80
VMEM_LIMIT = 56 * 1024 * 1024

C_FQ, C_FK, C_FV, C_DQ, C_DK, C_DV, C_HQ, C_HI, C_HG = (0, 256, 512, 768, 1024, 1280, 1536, 2048, 2560)
MAIN_W = 3072
GATE_W = HGRN_W + LANES


def _params(sem, vmem=VMEM_LIMIT, flags=None):
    return pltpu.CompilerParams(dimension_semantics=sem, vmem_limit_bytes=vmem, flags=flags)


def _split3(x):
    hi = x.astype(BF16)
    r = x - hi.astype(F32)
    mid = r.astype(BF16)
    lo = (r - mid.astype(F32)).astype(BF16)
    return hi, mid, lo


def _dot(a, b):
    return jnp.dot(a, b, preferred_element_type=F32)


def _dot_nt(a, b):
    return lax.dot_general(a, b, (((1,), (1,)), ((), ())), preferred_element_type=F32)


def _sigmoid(x):
    return 1.0 / (1.0 + jnp.exp(-x))


def _log_sigmoid(x):
    return jnp.minimum(x, 0.0) - jnp.log(1.0 + jnp.exp(-jnp.abs(x)))


def _layer_norm(x, g, b):
    mu = jnp.mean(x, axis=-1, keepdims=True)
    xc = x - mu
    var = jnp.mean(xc * xc, axis=-1, keepdims=True)
    return xc * lax.rsqrt(var + LN_EPS) * g + b


def _mm_kernel(x_ref, w_ref, o_ref):
    o_ref[...] = _dot(x_ref[...], w_ref[...]).astype(o_ref.dtype)


def _matmul(x, w, out_dtype, tm, tn, name):
    n, k = x.shape
    m = w.shape[1]
    return pl.pallas_call(
        _mm_kernel, grid=(n // tm, m // tn),
        in_specs=[pl.BlockSpec((tm, k), lambda i, j: (i, 0)), pl.BlockSpec((k, tn), lambda i, j: (0, j))],
        out_specs=pl.BlockSpec((tm, tn), lambda i, j: (i, j)),
        out_shape=jax.ShapeDtypeStruct((n, m), out_dtype),
        compiler_params=_params(("parallel", "arbitrary")), name=name)(x, w)


IN_PROJ_CHUNK = 512


def _project(x, wm_ref, wg_ref, pm_ref, pg_ref):
    for c0 in range(0, MAIN_W, IN_PROJ_CHUNK):
        cs = slice(c0, c0 + IN_PROJ_CHUNK)
        pm_ref[:, cs] = _dot(x, wm_ref[:, cs]).astype(BF16)
    pg_ref[...] = _dot(x, wg_ref[...])


def _in_proj_kernel(x_ref, wm_ref, wg_ref, pm_ref, pg_ref):
    _project(x_ref[...], wm_ref, wg_ref, pm_ref, pg_ref)


def _entry_in_proj_kernel(x_ref, g_ref, b_ref, wm_ref, wg_ref, h_ref, pm_ref, pg_ref):
    y = _layer_norm(x_ref[...], g_ref[...], b_ref[...])
    h_ref[...] = y
    _project(y.astype(BF16), wm_ref, wg_ref, pm_ref, pg_ref)


def _entry_in_proj(x, g, b, w_main, w_gate, tm):
    n, d = x.shape
    row = pl.BlockSpec((tm, d), lambda i: (i, 0))
    vec = pl.BlockSpec((1, d), lambda i: (0, 0))
    return pl.pallas_call(
        _entry_in_proj_kernel, grid=(n // tm,),
        in_specs=[row, vec, vec, pl.BlockSpec((d, MAIN_W), lambda i: (0, 0)),
                  pl.BlockSpec((d, GATE_W), lambda i: (0, 0))],
        out_specs=[row, pl.BlockSpec((tm, MAIN_W), lambda i: (i, 0)), pl.BlockSpec((tm, GATE_W), lambda i: (i, 0))],
        out_shape=[jax.ShapeDtypeStruct((n, d), F32), jax.ShapeDtypeStruct((n, MAIN_W), BF16),
                   jax.ShapeDtypeStruct((n, GATE_W), F32)],
        compiler_params=_params(("parallel",)), name="entry_in_proj")(x, g, b, w_main, w_gate)


def _in_proj(hb, w_main, w_gate, tm):
    n, d = hb.shape
    return pl.pallas_call(
        _in_proj_kernel, grid=(n // tm,),
        in_specs=[pl.BlockSpec((tm, d), lambda i: (i, 0)),
                  pl.BlockSpec((d, MAIN_W), lambda i: (0, 0)), pl.BlockSpec((d, GATE_W), lambda i: (0, 0))],
        out_specs=[pl.BlockSpec((tm, MAIN_W), lambda i: (i, 0)), pl.BlockSpec((tm, GATE_W), lambda i: (i, 0))],
        out_shape=[jax.ShapeDtypeStruct((n, MAIN_W), BF16), jax.ShapeDtypeStruct((n, GATE_W), F32)],
        compiler_params=_params(("parallel",)), name="in_proj")(hb, w_main, w_gate)


def _value_rows(v_ref, vt_ref, heads, dim):
    ts = v_ref.shape[0]
    vt = v_ref[...].astype(F32).T.astype(BF16)
    tail_row = lax.broadcasted_iota(jnp.int32, (V_ROWS - dim, ts), 0)
    tail = jnp.where(tail_row == 0, 1.0, 0.0).astype(BF16)
    for h in range(heads):
        vt_ref[0, h, 0:dim, :] = vt[h * dim:(h + 1) * dim]
        vt_ref[0, h, dim:V_ROWS, :] = tail


def _fox_prep_kernel(fq_ref, fk_ref, fv_ref, dv_ref, ff_ref, fb_ref, tri_ref, sel_ref, eq_ref, ek_ref, cq_ref,
                     ck_ref, qa_ref, ka_ref, fvt_ref, dvt_ref, carry_ref):
    @pl.when(pl.program_id(1) == 0)
    def _():
        carry_ref[...] = jnp.zeros_like(carry_ref)

    _value_rows(fv_ref, fvt_ref, FOX_HEADS, FOX_DIM)
    _value_rows(dv_ref, dvt_ref, DIFF_HEADS, DIFF_V_DIM)

    logf = _log_sigmoid(ff_ref[...] + fb_ref[...])
    tri = tri_ref[...]
    hi, mid, lo = _split3(logf)
    cum = _dot(tri, hi) + _dot(tri, mid) + _dot(tri, lo) + carry_ref[...]
    ts = cum.shape[0]
    carry_ref[...] = cum[ts - 1:ts, :]
    parts = jnp.concatenate(_split3(cum * LOG2E), axis=1)
    fq = fq_ref[...]
    fk = fk_ref[...]
    for h in range(FOX_HEADS):
        pair = slice((h // 2) * LANES, (h // 2 + 1) * LANES)
        sel = sel_ref[h % 2]
        qa = _dot(fq[:, pair], sel) + _dot(parts, eq_ref[h]) + cq_ref[...]
        ka = _dot(fk[:, pair], sel) + _dot(parts, ek_ref[h]) + ck_ref[...]
        qa_ref[0, h] = qa.astype(BF16)
        ka_ref[0, h] = ka.astype(BF16)


def _fox_prep_consts():
    sel = np.zeros((2, LANES, LANES), np.float32)
    for half in range(2):
        for d in range(FOX_DIM):
            sel[half, half * FOX_DIM + d, d] = 1.0
    eq = np.zeros((FOX_HEADS, 3 * LANES, LANES), np.float32)
    ek = np.zeros((FOX_HEADS, 3 * LANES, LANES), np.float32)
    for h in range(FOX_HEADS):
        for p in range(3):
            eq[h, p * LANES + h, FOX_DIM + p] = 1.0
            ek[h, p * LANES + h, FOX_DIM + 3 + p] = -1.0
    cq = np.zeros((1, LANES), np.float32)
    ck = np.zeros((1, LANES), np.float32)
    cq[0, FOX_DIM + 3:FOX_DIM + 6] = 1.0
    ck[0, FOX_DIM:FOX_DIM + 3] = 1.0
    return (jnp.asarray(sel, BF16), jnp.asarray(eq, BF16), jnp.asarray(ek, BF16), jnp.asarray(cq), jnp.asarray(ck))


def _fox_prep(p_main, g_gate, fb, batch, seq, ts):
    ns = seq // ts
    tri = jnp.asarray(np.tril(np.ones((ts, ts), np.float32)), BF16)
    sel, eq, ek, cq, ck = _fox_prep_consts()
    fb_pad = jnp.zeros((1, LANES), F32).at[0, :FOX_HEADS].set(fb)
    const2 = lambda shape: pl.BlockSpec(shape, lambda b, s: (0,) * len(shape))
    out_spec = pl.BlockSpec((1, FOX_HEADS, ts, LANES), lambda b, s: (b, 0, s, 0))
    out_sds = jax.ShapeDtypeStruct((batch, FOX_HEADS, seq, LANES), BF16)
    vt_spec = pl.BlockSpec((1, FOX_HEADS, V_ROWS, ts), lambda b, s: (b, 0, 0, s))
    vt_sds = jax.ShapeDtypeStruct((batch, FOX_HEADS, V_ROWS, seq), BF16)
    col = lambda c0: pl.BlockSpec((ts, 256), lambda b, s: (b * ns + s, c0 // 256))
    return pl.pallas_call(
        _fox_prep_kernel, grid=(batch, ns),
        in_specs=[col(C_FQ), col(C_FK), col(C_FV), col(C_DV),
                  pl.BlockSpec((ts, LANES), lambda b, s: (b * ns + s, HGRN_W // LANES)),
                  const2((1, LANES)), const2((ts, ts)), const2((2, LANES, LANES)),
                  const2((FOX_HEADS, 3 * LANES, LANES)), const2((FOX_HEADS, 3 * LANES, LANES)),
                  const2((1, LANES)), const2((1, LANES))],
        out_specs=[out_spec, out_spec, vt_spec, vt_spec], out_shape=[out_sds, out_sds, vt_sds, vt_sds],
        scratch_shapes=[pltpu.VMEM((1, LANES), F32)],
        compiler_params=_params(("arbitrary", "arbitrary")), name="fox_prep",
    )(p_main, p_main, p_main, p_main, g_gate, fb_pad, tri, sel, eq, ek, cq, ck)


def _flash_sweep(qi, n_streams, score_fn, value_fn, visible, sa_ref, sb_ref, m_ref, acc_ref):
    m_ref[...] = jnp.full(m_ref.shape, NEG_INIT, F32)
    acc_ref[...] = jnp.zeros(acc_ref.shape, F32)

    def consume(src_ref, st, j, masked):
        s = src_ref[st]
        if masked:
            s = jnp.where(visible, s, -jnp.inf)
        m = m_ref[st]
        m_new = jnp.maximum(m, jnp.max(s, axis=0, keepdims=True))
        p = jnp.exp2(s - m_new).astype(BF16)
        acc_ref[st] = jnp.exp2(m - m_new) * acc_ref[st] + _dot(value_fn(st, j), p)
        m_ref[st] = m_new

    def stage(dst_ref, src_ref, j_next, j_cur):
        for st in range(n_streams):
            dst_ref[st] = score_fn(st, j_next)
            consume(src_ref, st, j_cur, False)

    for st in range(n_streams):
        sa_ref[st] = score_fn(st, 0)

    def pair(p, carry):
        stage(sb_ref, sa_ref, 2 * p + 1, 2 * p)
        stage(sa_ref, sb_ref, 2 * p + 2, 2 * p + 1)
        return carry

    lax.fori_loop(0, qi // 2, pair, 0)

    @pl.when(qi % 2 == 0)
    def _():
        for st in range(n_streams):
            consume(sa_ref, st, qi, True)

    @pl.when(qi % 2 == 1)
    def _():
        stage(sb_ref, sa_ref, qi, qi - 1)
        for st in range(n_streams):
            consume(sb_ref, st, qi, True)


def _flash_scratch(n_streams, tile):
    return [pltpu.VMEM((n_streams, tile, tile), F32), pltpu.VMEM((n_streams, tile, tile), F32),
            pltpu.VMEM((n_streams, 1, tile), F32), pltpu.VMEM((n_streams, V_ROWS, tile), F32)]


def _finish(acc, dim):
    return acc[:dim] / acc[dim:dim + 1]


def _transposed(o, tile):
    dim = o.shape[0]
    padded = jnp.concatenate([o, jnp.zeros((LANES - dim, tile), F32)], axis=0)
    return padded.T[:, :dim]


def _key_block(j, tile):
    return pl.ds(pl.multiple_of(j * tile, tile), tile)


def _fox_attn_kernel(q_ref, k_ref, vt_ref, o_ref, qt_ref, sa_ref, sb_ref, m_ref, acc_ref, *, tile):
    qi = pl.program_id(1)
    for h in range(FOX_HEADS):
        qt_ref[h] = q_ref[0, h].astype(F32).T.astype(BF16)
    row = lax.broadcasted_iota(jnp.int32, (tile, tile), 0)
    col = lax.broadcasted_iota(jnp.int32, (tile, tile), 1)
    visible = row <= col
    _flash_sweep(qi, FOX_HEADS,
                 lambda h, j: _dot(k_ref[0, h, _key_block(j, tile), :], qt_ref[h]),
                 lambda h, j: vt_ref[0, h, :, _key_block(j, tile)],
                 visible, sa_ref, sb_ref, m_ref, acc_ref)
    o = jnp.concatenate([_finish(acc_ref[h], FOX_DIM) for h in range(FOX_HEADS)], axis=0)
    o_ref[...] = o.T.astype(o_ref.dtype)


def _fox_attn(q_aug, k_aug, vt_aug, tile):
    batch, heads, seq, _ = q_aug.shape
    return pl.pallas_call(
        functools.partial(_fox_attn_kernel, tile=tile), grid=(batch, seq // tile),
        in_specs=[pl.BlockSpec((1, heads, tile, LANES), lambda b, i: (b, 0, i, 0)),
                  pl.BlockSpec((1, heads, seq, LANES), lambda b, i: (b, 0, 0, 0)),
                  pl.BlockSpec((1, heads, V_ROWS, seq), lambda b, i: (b, 0, 0, 0))],
        out_specs=pl.BlockSpec((tile, heads * FOX_DIM), lambda b, i: (b * (seq // tile) + i, 0)),
        out_shape=jax.ShapeDtypeStruct((batch * seq, heads * FOX_DIM), BF16),
        scratch_shapes=[pltpu.VMEM((heads, LANES, tile), BF16)] + _flash_scratch(heads, tile),
        compiler_params=_params(("parallel", "arbitrary")), name="fox_attn")(q_aug, k_aug, vt_aug)


def _diff_attn_kernel(q_ref, k_ref, vt_ref, lam_ref, g_ref, o_ref, q12_ref, sa_ref, sb_ref, m_ref, acc_ref,
                      *, tile, lam_init):
    qi = pl.program_id(1)
    qt = q_ref[...].astype(F32).T
    feat = lax.broadcasted_iota(jnp.int32, qt.shape, 0)
    for st in range(2 * DIFF_HEADS):
        keep = (feat >= DIFF_QK_DIM * st) & (feat < DIFF_QK_DIM * (st + 1))
        q12_ref[st] = jnp.where(keep, qt, 0.0).astype(BF16)
    row = lax.broadcasted_iota(jnp.int32, (tile, tile), 0)
    col = lax.broadcasted_iota(jnp.int32, (tile, tile), 1)
    visible = (row // HGRN_CHUNK) <= (col // HGRN_CHUNK)
    _flash_sweep(qi, 2 * DIFF_HEADS,
                 lambda st, j: _dot(k_ref[_key_block(j, tile), :], q12_ref[st]),
                 lambda st, j: vt_ref[0, st // 2, :, _key_block(j, tile)],
                 visible, sa_ref, sb_ref, m_ref, acc_ref)

    lp = lam_ref[...]
    lam = (jnp.exp(jnp.sum(lp[0:1] * lp[1:2], axis=1, keepdims=True))
           - jnp.exp(jnp.sum(lp[2:3] * lp[3:4], axis=1, keepdims=True)) + lam_init)
    outs = []
    for h in range(DIFF_HEADS):
        o = _finish(acc_ref[2 * h], DIFF_V_DIM) - lam * _finish(acc_ref[2 * h + 1], DIFF_V_DIM)
        outs.append(o * lax.rsqrt(jnp.mean(o * o, axis=0, keepdims=True) + LN_EPS))
    y = jnp.concatenate(outs, axis=0).T * g_ref[...] * (1.0 - lam_init)
    o_ref[...] = y.astype(o_ref.dtype)


def _diff_attn(p_main, dvt_aug, lam_pack, norm_g, lam_init, batch, seq, tile):
    heads = DIFF_HEADS
    width = heads * DIFF_V_DIM
    nq = seq // tile
    return pl.pallas_call(
        functools.partial(_diff_attn_kernel, tile=tile, lam_init=lam_init), grid=(batch, nq),
        in_specs=[pl.BlockSpec((tile, width), lambda b, i: (b * nq + i, C_DQ // width)),
                  pl.BlockSpec((seq, width), lambda b, i: (b, C_DK // width)),
                  pl.BlockSpec((1, heads, V_ROWS, seq), lambda b, i: (b, 0, 0, 0)),
                  pl.BlockSpec((8, LANES), lambda b, i: (0, 0)),
                  pl.BlockSpec((1, width), lambda b, i: (0, 0))],
        out_specs=pl.BlockSpec((tile, width), lambda b, i: (b * nq + i, 0)),
        out_shape=jax.ShapeDtypeStruct((batch * seq, width), BF16),
        scratch_shapes=[pltpu.VMEM((2 * heads, width, tile), BF16)] + _flash_scratch(2 * heads, tile),
        compiler_params=_params(("parallel", "arbitrary")), name="diff_attn",
    )(p_main, p_main, dvt_aug, lam_pack, norm_g)


HGRN_SAFE_SPAN = 80.0


def _hgrn_kernel(hq_ref, hi_ref, hg_ref, hf_ref, lb_ref, g_ref, tri_ref, o_ref, state_ref, score_ref,
                 q_ref, k_ref, bc_ref, *, layer, tc):
    @pl.when(pl.program_id(1) == 0)
    def _():
        state_ref[...] = jnp.zeros_like(state_ref)

    lb_all = lb_ref[...]
    e = jnp.exp(lb_all - jnp.max(lb_all, axis=0, keepdims=True))
    prob = e / jnp.sum(e, axis=0, keepdims=True)
    lb_row = jnp.zeros((1, HGRN_W), F32)
    for j in range(1, layer + 1):
        lb_row = lb_row + prob[j:j + 1]
    lb_row = jnp.maximum(lb_row, 0.0)

    c = HGRN_CHUNK
    n_chunks = tc // c
    half = c // 2 - 1
    r_i = lax.broadcasted_iota(jnp.int32, (c, c), 0)
    c_i = lax.broadcasted_iota(jnp.int32, (c, c), 1)
    causal = r_i >= c_i

    log_lb = jnp.log(lb_row)
    log_1m = jnp.log(1.0 - lb_row)
    z = hf_ref[...]
    ez = jnp.exp(-jnp.abs(z))
    b_term = log_1m + jnp.minimum(z, 0.0) - jnp.log(1.0 + ez)
    logf = jnp.maximum(log_lb, b_term) + jnp.log(1.0 + jnp.exp(-jnp.abs(log_lb - b_term)))
    kk = (1.0 - lb_row) * jnp.where(z >= 0.0, ez, 1.0) / (1.0 + ez)
    xq = hq_ref[...].astype(F32)
    q = xq * (0.5 + 0.5 * jnp.tanh(0.5 * xq))
    xg = hg_ref[...].astype(F32)
    gate = xg * (0.5 + 0.5 * jnp.tanh(0.5 * xg))
    v = hi_ref[...]
    vt = v.astype(F32).T.astype(BF16)

    tri = tri_ref[...]
    g_hi, g_mid, g_lo = _split3(logf)
    bc = _dot(tri, g_hi) + _dot(tri, g_mid) + _dot(tri, g_lo)

    rows = lambda r: jnp.concatenate(
        [jnp.broadcast_to(bc[ci * c + r:ci * c + r + 1], (c, HGRN_W)) for ci in range(n_chunks)], axis=0)
    ref_full = rows(half)
    last_full = rows(c - 1)
    span = bc - ref_full
    q_mid = (q * jnp.exp(span)).astype(BF16)
    k_mid = (kk * jnp.exp(-span)).astype(BF16)
    q_in = (q * jnp.exp(bc)).astype(BF16)
    k_out = (kk * jnp.exp(last_full - bc)).astype(BF16)

    units = [(ci, h) for ci in range(n_chunks) for h in range(HGRN_HEADS)]
    sl = lambda ci, h: (slice(ci * c, (ci + 1) * c), slice(h * HGRN_DIM, (h + 1) * HGRN_DIM))
    for idx, u in enumerate(units):
        score_ref[idx] = jnp.where(causal, _dot_nt(q_mid[sl(*u)], k_mid[sl(*u)]), 0.0)

    @pl.when(jnp.max(jnp.abs(span)) > HGRN_SAFE_SPAN)
    def _():
        q_ref[...] = q
        k_ref[...] = kk
        bc_ref[...] = bc
        for idx, (ci, h) in enumerate(units):
            rs, hs = sl(ci, h)
            q_c = q_ref[rs, hs]
            bc_c = bc_ref[rs, hs]

            def columns(a, acc):
                base = pl.multiple_of(ci * c + a * 8, 8)
                k_8 = k_ref[pl.ds(base, 8), hs]
                b_8 = bc_ref[pl.ds(base, 8), hs]
                for j in range(8):
                    decay = jnp.exp(jnp.minimum(bc_c - b_8[j:j + 1], 0.0))
                    col = jnp.sum(q_c * (k_8[j:j + 1] * decay), axis=1, keepdims=True)
                    acc = jnp.where(c_i == a * 8 + j, col, acc)
                return acc

            exact = lax.fori_loop(0, c // 8, columns, jnp.zeros((c, c), F32))
            score_ref[idx] = jnp.where(causal, exact, 0.0)

    intra = {u: _dot(score_ref[idx].astype(BF16), v[sl(*u)]) for idx, u in enumerate(units)}
    updates = {(ci, h): _dot(vt[h * HGRN_DIM:(h + 1) * HGRN_DIM, ci * c:(ci + 1) * c], k_out[sl(ci, h)])
               for ci, h in units}

    states = [state_ref[h] for h in range(HGRN_HEADS)]
    for ci in range(n_chunks):
        for h in range(HGRN_HEADS):
            rs, hs = sl(ci, h)
            o = intra[(ci, h)] + _dot_nt(q_in[rs, hs], states[h].astype(BF16))
            decay = jnp.exp(bc[ci * c + c - 1:ci * c + c, hs])
            states[h] = states[h] * decay + updates[(ci, h)]
            y = o * lax.rsqrt(jnp.mean(o * o, axis=1, keepdims=True) + LN_EPS) * g_ref[...] * gate[rs, hs]
            o_ref[rs, hs] = y.astype(o_ref.dtype)
    for h in range(HGRN_HEADS):
        state_ref[h] = states[h]


def _hgrn(p_main, g_gate, hgrn_lb, norm_g, layer, batch, seq, tc):
    ns = seq // tc
    chunk_tri = np.tril(np.ones((HGRN_CHUNK, HGRN_CHUNK), np.float32))
    tri = jnp.asarray(np.kron(np.eye(tc // HGRN_CHUNK, dtype=np.float32), chunk_tri), BF16)
    col = lambda c0: pl.BlockSpec((tc, HGRN_W), lambda b, s: (b * ns + s, c0 // HGRN_W))
    return pl.pallas_call(
        functools.partial(_hgrn_kernel, layer=layer, tc=tc), grid=(batch, ns),
        in_specs=[col(C_HQ), col(C_HI), col(C_HG),
                  pl.BlockSpec((tc, HGRN_W), lambda b, s: (b * ns + s, 0)),
                  pl.BlockSpec((DEPTH, HGRN_W), lambda b, s: (0, 0)),
                  pl.BlockSpec((1, HGRN_DIM), lambda b, s: (0, 0)),
                  pl.BlockSpec((tc, tc), lambda b, s: (0, 0))],
        out_specs=pl.BlockSpec((tc, HGRN_W), lambda b, s: (b * ns + s, 0)),
        out_shape=jax.ShapeDtypeStruct((batch * seq, HGRN_W), BF16),
        scratch_shapes=[pltpu.VMEM((HGRN_HEADS, HGRN_DIM, HGRN_DIM), F32),
                        pltpu.VMEM((tc // HGRN_CHUNK * HGRN_HEADS, HGRN_CHUNK, HGRN_CHUNK), F32),
                        pltpu.VMEM((tc, HGRN_W), F32), pltpu.VMEM((tc, HGRN_W), F32), pltpu.VMEM((tc, HGRN_W), F32)],
        compiler_params=_params(("parallel", "arbitrary")), name="hgrn2",
    )(p_main, p_main, p_main, g_gate, hgrn_lb, norm_g, tri)


def _proj_ln_kernel(yf_ref, yd_ref, yh_ref, w_ref, h_ref, g_ref, b_ref, ho_ref, hbo_ref):
    fw = yf_ref.shape[1]
    dw = fw + yd_ref.shape[1]
    mix = _dot(yf_ref[...], w_ref[0:fw]) + _dot(yd_ref[...], w_ref[fw:dw]) + _dot(yh_ref[...], w_ref[dw:])
    out = _layer_norm(ALPHA * h_ref[...] + mix, g_ref[...], b_ref[...])
    ho_ref[...] = out
    hbo_ref[...] = out.astype(BF16)


def _proj_ln(y_fox, y_diff, y_hgrn, w, h, g, b, tm):
    n, d = h.shape
    row = lambda width: pl.BlockSpec((tm, width), lambda i: (i, 0))
    vec = pl.BlockSpec((1, d), lambda i: (0, 0))
    return pl.pallas_call(
        _proj_ln_kernel, grid=(n // tm,),
        in_specs=[row(y_fox.shape[1]), row(y_diff.shape[1]), row(y_hgrn.shape[1]),
                  pl.BlockSpec(w.shape, lambda i: (0, 0)), row(d), vec, vec],
        out_specs=[row(d), row(d)],
        out_shape=[jax.ShapeDtypeStruct((n, d), F32), jax.ShapeDtypeStruct((n, d), BF16)],
        compiler_params=_params(("parallel",)), name="out_proj_ln")(y_fox, y_diff, y_hgrn, w, h, g, b)


def _mem_attn_kernel(hb_ref, h_ref, wq_ref, k_ref, v_ref, wo_ref, g_ref, b_ref, ho_ref):
    q = _dot(hb_ref[...], wq_ref[...])
    outs = []
    for hd in range(MEM_HEADS):
        hs = slice(hd * MEM_DIM, (hd + 1) * MEM_DIM)
        s = _dot_nt(q[:, hs].astype(BF16), k_ref[:, hs])
        p = jnp.exp2(s - jnp.max(s, axis=1, keepdims=True))
        o = _dot(p.astype(BF16), v_ref[:, hs]) / jnp.sum(p, axis=1, keepdims=True)
        outs.append(o.astype(BF16))
    att = jnp.concatenate(outs, axis=1)
    z = ALPHA * h_ref[...] + _dot(att, wo_ref[...])
    ho_ref[...] = _layer_norm(z, g_ref[...], b_ref[...])


def _mem_attn(hb, h, wq, kmem, vmem, wo, g, b, batch, seq, tm):
    n, d = h.shape
    ns = seq // tm
    mem_len = kmem.shape[0] // batch
    row = pl.BlockSpec((tm, d), lambda bi, s: (bi * ns + s, 0))
    full = pl.BlockSpec((d, d), lambda bi, s: (0, 0))
    mem = pl.BlockSpec((mem_len, d), lambda bi, s: (bi, 0))
    vec = pl.BlockSpec((1, d), lambda bi, s: (0, 0))
    return pl.pallas_call(
        _mem_attn_kernel, grid=(batch, ns),
        in_specs=[row, row, full, mem, mem, full, vec, vec], out_specs=row,
        out_shape=jax.ShapeDtypeStruct((n, d), F32),
        compiler_params=_params(("parallel", "arbitrary")), name="mem_attn")(hb, h, wq, kmem, vmem, wo, g, b)


def _top2_sum(a, b, c, d):
    hi1, lo1 = jnp.maximum(a, b), jnp.minimum(a, b)
    hi2, lo2 = jnp.maximum(c, d), jnp.minimum(c, d)
    return jnp.maximum(hi1, hi2) + jnp.maximum(jnp.minimum(hi1, hi2), jnp.maximum(lo1, lo2))


def _router_kernel(h_ref, rw_ref, rb_ref, gate_ref, pick_ref, count_ref):
    xh, xm, xl = _split3(h_ref[...])
    rw = rw_ref[...]
    ph, pm, pl_ = _dot_nt(rw, xh), _dot_nt(rw, xm), _dot_nt(rw, xl)
    ne = N_EXPERTS
    logits = (ph[0:ne] + ph[ne:2 * ne] + pm[0:ne]) + (ph[2 * ne:] + pm[ne:2 * ne] + pl_[0:ne])
    scores = _sigmoid(logits)
    biased = scores + rb_ref[...]
    s_rows = [scores[e:e + 1] for e in range(N_EXPERTS)]
    b_rows = [biased[e:e + 1] for e in range(N_EXPERTS)]
    group_score = [_top2_sum(*b_rows[4 * g:4 * g + 4]) for g in range(N_GROUPS)]
    best = group_score[0]
    best_idx = jnp.zeros_like(best, dtype=jnp.int32)
    for g in range(1, N_GROUPS):
        better = group_score[g] > best
        best_idx = jnp.where(better, g, best_idx)
        best = jnp.where(better, group_score[g], best)
    picked = []
    for e in range(N_EXPERTS):
        g = e // EXPERTS_PER_GROUP
        rank = jnp.zeros_like(best)
        for j in range(4 * g, 4 * g + 4):
            if j == e:
                continue
            ahead = (b_rows[j] > b_rows[e]) | (b_rows[j] == b_rows[e]) if j < e else (b_rows[j] > b_rows[e])
            rank = rank + jnp.where(ahead, 1.0, 0.0)
        picked.append((best_idx == g) & (rank < 2.0))
    denom = jnp.zeros_like(best)
    for e in range(N_EXPERTS):
        denom = denom + jnp.where(picked[e], s_rows[e], 0.0)
    gate_ref[...] = jnp.zeros_like(gate_ref)
    pick_ref[...] = jnp.zeros_like(pick_ref)
    for slot in range(EXPERTS_PER_GROUP):
        gate = jnp.zeros_like(best)
        for g in range(N_GROUPS):
            e = g * EXPERTS_PER_GROUP + slot
            gate = gate + jnp.where(picked[e], s_rows[e] / denom, 0.0)
        gate_ref[slot:slot + 1, :] = gate
    for g in range(N_GROUPS):
        pick_ref[g:g + 1, :] = jnp.where(best_idx == g, 1.0, 0.0)

    @pl.when(pl.program_id(0) == 0)
    def _():
        count_ref[...] = jnp.zeros_like(count_ref)

    count_ref[...] += jnp.broadcast_to(jnp.sum(pick_ref[...], axis=1, keepdims=True), count_ref.shape)


ROUTE_ROWS = 8


def _router(h, rw_parts, rb_col, tm):
    n, d = h.shape
    rows = pl.BlockSpec((ROUTE_ROWS, tm), lambda i: (0, i))
    return pl.pallas_call(
        _router_kernel, grid=(n // tm,),
        in_specs=[pl.BlockSpec((tm, d), lambda i: (i, 0)),
                  pl.BlockSpec((3 * N_EXPERTS, d), lambda i: (0, 0)),
                  pl.BlockSpec((N_EXPERTS, 1), lambda i: (0, 0))],
        out_specs=[rows, rows, pl.BlockSpec((ROUTE_ROWS, LANES), lambda i: (0, 0))],
        out_shape=[jax.ShapeDtypeStruct((ROUTE_ROWS, n), F32), jax.ShapeDtypeStruct((ROUTE_ROWS, n), F32),
                   jax.ShapeDtypeStruct((ROUTE_ROWS, LANES), F32)],
        compiler_params=_params(("arbitrary",)), name="router")(h, rw_parts, rb_col)


def _moe_pos_kernel(pick_ref, gate_ref, start_ref, upper_ref, pos_ref, gcol_ref, carry_ref, rows_ref):
    @pl.when(pl.program_id(0) == 0)
    def _():
        carry_ref[...] = jnp.zeros_like(carry_ref)

    pick = pick_ref[...]
    rank = _dot(pick.astype(BF16), upper_ref[...]) + carry_ref[:, 0:1]
    slot = start_ref[...] + rank
    pos_ref[...] = jnp.sum(pick * slot, axis=0, keepdims=True).astype(jnp.int32)
    rows_ref[...] = jnp.zeros_like(rows_ref)
    rows_ref[0:ROUTE_ROWS, :] = gate_ref[...]
    gcol_ref[...] = rows_ref[...].T
    carry_ref[...] += jnp.broadcast_to(jnp.sum(pick, axis=1, keepdims=True), carry_ref.shape)


def _moe_pos(picks, gates, starts_col, tm):
    n = picks.shape[1]
    upper = jnp.asarray(np.triu(np.ones((tm, tm), np.float32), 1), BF16)
    rows = pl.BlockSpec((ROUTE_ROWS, tm), lambda i: (0, i))
    return pl.pallas_call(
        _moe_pos_kernel, grid=(n // tm,),
        in_specs=[rows, rows, pl.BlockSpec((ROUTE_ROWS, 1), lambda i: (0, 0)),
                  pl.BlockSpec((tm, tm), lambda i: (0, 0))],
        out_specs=[pl.BlockSpec((1, tm), lambda i: (0, i)), pl.BlockSpec((tm, LANES), lambda i: (i, 0))],
        out_shape=[jax.ShapeDtypeStruct((1, n), jnp.int32), jax.ShapeDtypeStruct((n, LANES), F32)],
        scratch_shapes=[pltpu.VMEM((ROUTE_ROWS, LANES), F32), pltpu.VMEM((LANES, tm), F32)],
        compiler_params=_params(("arbitrary",)), name="moe_pos")(picks, gates, starts_col, upper)


def _row_copy(src_ref, src_row, dst_ref, dst_row, sem):
    return pltpu.make_async_copy(src_ref.at[pl.ds(src_row, 1)], dst_ref.at[pl.ds(dst_row, 1)], sem)


DMA_WAIT_UNROLL = 8


def _wait_rows(src_ref, dst_ref, sem, n_rows):
    def drain(i, carry):
        _row_copy(src_ref, 0, dst_ref, 0, sem).wait()
        return carry

    lax.fori_loop(0, n_rows, drain, 0, unroll=DMA_WAIT_UNROLL)


def _dispatch_kernel(pos_ref, tail_ref, used_ref, h_ref, gcol_ref, xs_ref, cat_ref, zero_ref, sem, *, tm, tile):
    i = pl.program_id(0)
    d = h_ref.shape[1]

    @pl.when(i == 0)
    def _():
        zero_ref[...] = jnp.zeros_like(zero_ref)

        def fill(row0):
            copy = pltpu.make_async_copy(zero_ref, xs_ref.at[pl.ds(pl.multiple_of(row0, tile), tile)], sem.at[2])
            copy.start()
            copy.wait()

        for g in range(N_GROUPS):
            @pl.when(tail_ref[g] >= 0)
            def _():
                fill(tail_ref[g])

        def fill_unused(t, carry):
            fill(t * tile)
            return carry

        lax.fori_loop(used_ref[0], xs_ref.shape[0] // tile, fill_unused, 0)

    def stage_and_issue(slot):
        stage = cat_ref.at[slot]
        stage[:, 0:d] = h_ref[...]
        stage[:, d:] = gcol_ref[...]

        def issue(j, carry):
            for k in range(2):
                r = 2 * j + k
                _row_copy(stage, r, xs_ref, pos_ref[r], sem.at[slot]).start(priority=k)
            return carry

        lax.fori_loop(0, tm // 2, issue, 0)

    for slot in range(2):
        @pl.when(i % 2 == slot)
        def _():
            stage_and_issue(slot)

            @pl.when(i > 0)
            def _():
                _wait_rows(cat_ref.at[1 - slot], xs_ref, sem.at[1 - slot], tm)

            @pl.when(i == pl.num_programs(0) - 1)
            def _():
                _wait_rows(cat_ref.at[slot], xs_ref, sem.at[slot], tm)


def _dispatch(h, gcol, pos, tails, n_used, n_rows, tm, tile):
    n, d = h.shape
    width = d + LANES
    whole_smem = pl.BlockSpec(memory_space=pltpu.SMEM)
    return pl.pallas_call(
        functools.partial(_dispatch_kernel, tm=tm, tile=tile), grid=(n // tm,),
        in_specs=[pl.BlockSpec((tm,), lambda i: (i,), memory_space=pltpu.SMEM), whole_smem, whole_smem,
                  pl.BlockSpec((tm, d), lambda i: (i, 0)), pl.BlockSpec((tm, LANES), lambda i: (i, 0))],
        out_specs=pl.BlockSpec(memory_space=pl.ANY),
        out_shape=jax.ShapeDtypeStruct((n_rows, width), F32),
        scratch_shapes=[pltpu.VMEM((2, tm, width), F32), pltpu.VMEM((tile, width), F32),
                        pltpu.SemaphoreType.DMA((3,))],
        compiler_params=_params(("arbitrary",)), name="moe_dispatch")(pos, tails, n_used, h, gcol)


def _expert_kernel(src_ref, grp_ref, used_ref, xs_ref, w1_ref, w3_ref, w2_ref, ys_ref):
    t = pl.program_id(0)
    d = ys_ref.shape[1]

    @pl.when(t < used_ref[0])
    def _():
        x = xs_ref[:, 0:d].astype(BF16)
        gates = xs_ref[:, d:]
        acc = jnp.zeros(ys_ref.shape, F32)
        for e in range(EXPERTS_PER_GROUP):
            a = _dot(x, w1_ref[e])
            hid = a * (0.5 + 0.5 * jnp.tanh(0.5 * a)) * _dot(x, w3_ref[e])
            acc = acc + _dot((hid * gates[:, e:e + 1]).astype(BF16), w2_ref[e])
        ys_ref[...] = acc

    @pl.when(t >= used_ref[0])
    def _():
        ys_ref[...] = jnp.zeros_like(ys_ref)


def _experts(xs, tile_src, tile_grp, n_used, w1, w3, w2, tile):
    n_rows, width = xs.shape
    _, d, ff = w1.shape
    per = EXPERTS_PER_GROUP
    grid_spec = pltpu.PrefetchScalarGridSpec(
        num_scalar_prefetch=3, grid=(n_rows // tile,),
        in_specs=[pl.BlockSpec((tile, width), lambda t, src, grp, used: (src[t], 0)),
                  pl.BlockSpec((per, d, ff), lambda t, src, grp, used: (grp[t], 0, 0)),
                  pl.BlockSpec((per, d, ff), lambda t, src, grp, used: (grp[t], 0, 0)),
                  pl.BlockSpec((per, ff, d), lambda t, src, grp, used: (grp[t], 0, 0))],
        out_specs=pl.BlockSpec((tile, d), lambda t, src, grp, used: (t, 0)))
    return pl.pallas_call(
        _expert_kernel, grid_spec=grid_spec, out_shape=jax.ShapeDtypeStruct((n_rows, d), F32),
        compiler_params=_params(("arbitrary",)), name="moe_experts",
    )(tile_src, tile_grp, n_used, xs, w1, w3, w2)


def _combine_kernel(pos_ref, nxt_ref, ys_ref, h_ref, g_ref, b_ref, ho_ref, hbo_ref, y_ref, sem, *, tm):
    i = pl.program_id(0)

    def fetch(idx_ref, slot):
        def issue(j, carry):
            for k in range(2):
                r = 2 * j + k
                _row_copy(ys_ref, idx_ref[r], y_ref.at[slot], r, sem.at[slot]).start(priority=k)
            return carry

        lax.fori_loop(0, tm // 2, issue, 0)

    @pl.when(i == 0)
    def _():
        fetch(pos_ref, 0)

    for slot in range(2):
        @pl.when(i % 2 == slot)
        def _():
            @pl.when(i + 1 < pl.num_programs(0))
            def _():
                fetch(nxt_ref, 1 - slot)

            _wait_rows(ys_ref, y_ref.at[slot], sem.at[slot], tm)
            out = _layer_norm(ALPHA * h_ref[...] + y_ref[slot], g_ref[...], b_ref[...])
            ho_ref[...] = out
            hbo_ref[...] = out.astype(BF16)


def _combine(ys, pos, h, g, b, tm):
    n, d = h.shape
    last = n // tm - 1
    row = pl.BlockSpec((tm, d), lambda i: (i, 0))
    vec = pl.BlockSpec((1, d), lambda i: (0, 0))
    return pl.pallas_call(
        functools.partial(_combine_kernel, tm=tm), grid=(n // tm,),
        in_specs=[pl.BlockSpec((tm,), lambda i: (i,), memory_space=pltpu.SMEM),
                  pl.BlockSpec((tm,), lambda i: (jnp.minimum(i + 1, last),), memory_space=pltpu.SMEM),
                  pl.BlockSpec(memory_space=pl.ANY), row, vec, vec],
        out_specs=[row, row],
        out_shape=[jax.ShapeDtypeStruct((n, d), F32), jax.ShapeDtypeStruct((n, d), BF16)],
        scratch_shapes=[pltpu.VMEM((2, tm, d), F32), pltpu.SemaphoreType.DMA((2,))],
        compiler_params=_params(("arbitrary",)), name="moe_combine")(pos, pos, ys, h, g, b)


def _routed_moe(h, rw_parts, rb_col, w1s, w3s, w2s, g, b, tile):
    n, d = h.shape
    gates, picks, counts = _router(h, rw_parts, rb_col, min(2048, n))
    count = counts[:, 0].astype(jnp.int32)
    padded = (count + tile - 1) // tile * tile
    ends = jnp.cumsum(padded)
    starts = ends - padded
    n_tiles = n // tile + N_GROUPS
    n_used = (ends[-1] // tile).reshape(1).astype(jnp.int32)
    tile_src = jnp.minimum(jnp.arange(n_tiles, dtype=jnp.int32), n_used - 1)
    tile_grp = jnp.minimum(jnp.sum((tile_src[:, None] * tile >= ends[None, :N_GROUPS]).astype(jnp.int32), axis=1),
                           N_GROUPS - 1)
    tails = jnp.where(padded > 0, ends - tile, -1).astype(jnp.int32)

    pos, gcol = _moe_pos(picks, gates, starts.astype(F32).reshape(ROUTE_ROWS, 1), min(512, n))
    xs = _dispatch(h, gcol, pos[0], tails, n_used, n_tiles * tile, min(512, n), tile)
    ys = _experts(xs, tile_src, tile_grp, n_used, w1s, w3s, w2s, tile)
    return _combine(ys, pos[0], h, g, b, min(256, n))


def _pack_in_weights(w_in):
    sizes = (256, 256, 256, FOX_HEADS, 256, 256, 256, HGRN_W, HGRN_W, HGRN_W, HGRN_W)
    offs = np.cumsum((0,) + sizes)
    fq, fk, fv, ff, dq, dk, dv, hq, hf, hi, hg = (w_in[:, offs[i]:offs[i + 1]] for i in range(len(sizes)))
    fox_scale = FOX_DIM ** -0.5 * LOG2E
    diff_scale = DIFF_QK_DIM ** -0.5 * LOG2E
    main = jnp.concatenate([fq * fox_scale, fk, fv, dq * diff_scale, dk, dv, hq, hi, hg], axis=1).astype(BF16)
    ff_pad = jnp.zeros((w_in.shape[0], LANES), w_in.dtype).at[:, :FOX_HEADS].set(ff)
    gate = jnp.concatenate([hf, ff_pad], axis=1).astype(BF16)
    return main, gate


def kernel(x, mem, ln_in_g, ln_in_b, w_in, fox_fb, lam_q1, lam_k1, lam_q2, lam_k2, diff_norm_g, hgrn_lb,
           hgrn_norm_g, w_out, mem_wq, mem_wk, mem_wv, mem_wo, router_w, router_b, w1, w3, w2, ln_g, ln_b):
    batch, seq, d = x.shape
    n = batch * seq
    tm = min(512, seq)
    attn_tile = min(512, seq)
    vec = lambda v: v.reshape(1, -1).astype(F32)

    memb = mem.reshape(-1, d).astype(BF16)
    rw_parts = jnp.concatenate(_split3(router_w.T.astype(F32)), axis=0)
    rb_col = router_b.reshape(N_EXPERTS, 1).astype(F32)
    hgrn_lb = hgrn_lb.astype(F32)

    for i in range(DEPTH):
        lam_init = 0.8 - 0.6 * math.exp(-0.3 * i)
        w_main, w_gate = _pack_in_weights(w_in[i])
        if i == 0:
            h, p_main, g_gate = _entry_in_proj(x.reshape(n, d), vec(ln_in_g), vec(ln_in_b), w_main, w_gate, tm)
        else:
            p_main, g_gate = _in_proj(hb, w_main, w_gate, tm)

        q_aug, k_aug, fvt, dvt = _fox_prep(p_main, g_gate, fox_fb[i].astype(F32), batch, seq, min(512, seq))
        y_fox = _fox_attn(q_aug, k_aug, fvt, attn_tile)

        lam_pack = jnp.zeros((8, LANES), F32)
        for r, v in enumerate((lam_q1[i], lam_k1[i], lam_q2[i], lam_k2[i])):
            lam_pack = lam_pack.at[r, :DIFF_QK_DIM].set(v.astype(F32))
        diff_g = jnp.tile(vec(diff_norm_g[i]), (1, DIFF_HEADS))
        y_diff = _diff_attn(p_main, dvt, lam_pack, diff_g, lam_init, batch, seq, attn_tile)

        y_hgrn = _hgrn(p_main, g_gate, hgrn_lb, vec(hgrn_norm_g[i]), i, batch, seq, min(256, seq))
        h, hb = _proj_ln(y_fox, y_diff, y_hgrn, w_out[i].astype(BF16), h, vec(ln_g[i, 0]), vec(ln_b[i, 0]), tm)

        kmem = _matmul(memb, mem_wk[i].astype(BF16), BF16, memb.shape[0], 512, "mem_k")
        vmem = _matmul(memb, mem_wv[i].astype(BF16), BF16, memb.shape[0], 512, "mem_v")
        wq = (mem_wq[i] * (MEM_DIM ** -0.5 * LOG2E)).astype(BF16)
        h = _mem_attn(hb, h, wq, kmem, vmem, mem_wo[i].astype(BF16), vec(ln_g[i, 1]), vec(ln_b[i, 1]),
                          batch, seq, tm)

        h, hb = _routed_moe(h, rw_parts, rb_col, w1[i].astype(BF16), w3[i].astype(BF16), w2[i].astype(BF16),
                            vec(ln_g[i, 2]), vec(ln_b[i, 2]), min(512, n // 8))
    return h.reshape(batch, seq, d)
```
